```python
import jax, jax.numpy as jnp
from jax import lax
import numpy as np

D_MODEL = 1024
BATCH = 8
SEQ = 4096
DEPTH = 4

HEAD_DIM = 64
D_MIX = D_MODEL
D_FF = 2816
PLE_DIM = 256
ROPE_THETA = 10000.0
LN_EPS = 1e-5
RMS_EPS = 1e-6
NEG_INF = -1e30
LOG_FLOOR = 1e-30
ATTN_QBLK = 128
NSA_WIDTH = D_MIX // 4
NSA_HEADS = NSA_WIDTH // HEAD_DIM
NSA_KV = 1
CMP_LEN = 32
CMP_STRIDE = 16
CMP_HID = 256
SEL_BLOCK = 64
SEL_TOPN = 16
SEL_FORCE = 1e4
NSA_WINDOW = 512
NSA_QBLK = SEL_BLOCK
SWA_WIDTH = D_MIX // 4
SWA_HEADS = SWA_WIDTH // HEAD_DIM
SWA_KV = 2
SWA_WINDOW = 128
HGRN_WIDTH = D_MIX // 2
HGRN_HEADS = 4
HGRN_DIM = HGRN_WIDTH // HGRN_HEADS
HGRN_CHUNK = 64
N_IN = (NSA_WIDTH + 6 * NSA_KV * HEAD_DIM + 3 * NSA_HEADS
        + SWA_WIDTH + 2 * SWA_KV * HEAD_DIM + 4 * HGRN_WIDTH)

kernel_name = 'hybrid_nsa_swa_hgrn2_macaron_deepnorm'


def layer_norm(x, g, b):
    xf = x.astype(jnp.float32)
    mu = jnp.mean(xf, axis=-1, keepdims=True)
    var = jnp.mean(jnp.square(xf - mu), axis=-1, keepdims=True)
    return ((xf - mu) * lax.rsqrt(var + LN_EPS) * g + b).astype(x.dtype)


def swiglu(x, w_gu, w_down):
    a, u = jnp.split(x @ w_gu, 2, axis=-1)
    return (jax.nn.silu(a) * u) @ w_down


def rope_tables(S):
    inv = 1.0 / (ROPE_THETA ** (jnp.arange(0, HEAD_DIM, 2, dtype=jnp.float32) / HEAD_DIM))
    ang = jnp.arange(S, dtype=jnp.float32)[:, None] * inv[None, :]
    return jnp.cos(ang), jnp.sin(ang)


def apply_rope(x, cos, sin):
    x1, x2 = jnp.split(x, 2, axis=-1)
    c = cos[None, :, None, :].astype(x.dtype)
    s = sin[None, :, None, :].astype(x.dtype)
    return jnp.concatenate([x1 * c - x2 * s, x1 * s + x2 * c], axis=-1)


def banded_attention(q, k, v, window, sinks=None):
    B, S, Hq, hd = q.shape
    Hk = k.shape[2]
    G = Hq // Hk
    nb = S // ATTN_QBLK
    span = ATTN_QBLK + window
    kp = jnp.pad(k, ((0, 0), (window, 0), (0, 0), (0, 0)))
    vp = jnp.pad(v, ((0, 0), (window, 0), (0, 0), (0, 0)))
    qb = q.reshape(B, nb, ATTN_QBLK, Hk, G, hd).transpose(1, 0, 2, 3, 4, 5)
    rel_k = jnp.arange(span)[None, :] - window
    diff = jnp.arange(ATTN_QBLK)[:, None] - rel_k
    band = (diff >= 0) & (diff < window)
    scale = hd ** -0.5

    def block(args):
        qi, bi = args
        start = bi * ATTN_QBLK
        kk = lax.dynamic_slice_in_dim(kp, start, span, axis=1)
        vv = lax.dynamic_slice_in_dim(vp, start, span, axis=1)
        s = jnp.einsum('bqhgd,bkhd->bhgqk', qi, kk).astype(jnp.float32) * scale
        mask = band & ((start + rel_k) >= 0)
        s = jnp.where(mask, s, NEG_INF)
        if sinks is None:
            pr = jax.nn.softmax(s, axis=-1)
        else:
            sk = jnp.broadcast_to(sinks.astype(jnp.float32).reshape(1, Hk, G, 1, 1), s.shape[:-1] + (1,))
            pr = jax.nn.softmax(jnp.concatenate([s, sk], axis=-1), axis=-1)[..., :-1]
        return jnp.einsum('bhgqk,bkhd->bqhgd', pr.astype(vv.dtype), vv)

    out = lax.map(block, (qb, jnp.arange(nb)))
    return out.transpose(1, 0, 2, 3, 4, 5).reshape(B, S, Hq, hd)


def nsa_mixer(xq, k_cmp, v_cmp, k_slc, v_slc, k_win, v_win, gate_logit,
              cmp_pos, wk1, wk2, wv1, wv2, cos, sin):
    B, S, _ = xq.shape
    dt = xq.dtype
    q = xq.reshape(B, S, NSA_HEADS, HEAD_DIM)
    q_r = apply_rope(q, cos, sin)
    scale = HEAD_DIM ** -0.5
    n_cmp = (S - CMP_LEN) // CMP_STRIDE + 1
    tok = jnp.arange(n_cmp)[:, None] * CMP_STRIDE + jnp.arange(CMP_LEN)[None, :]

    def compress(t, w1, w2):
        blocks = t[:, tok] + cmp_pos
        return jax.nn.silu(blocks.reshape(B, n_cmp, CMP_LEN * HEAD_DIM) @ w1) @ w2

    kc = compress(k_cmp, wk1, wk2)
    vc = compress(v_cmp, wv1, wv2)
    cmp_end = jnp.arange(n_cmp) * CMP_STRIDE + CMP_LEN - 1
    n_sel = S // SEL_BLOCK
    top_n = min(SEL_TOPN, n_sel)
    ci = jnp.arange(n_cmp)[:, None] * CMP_STRIDE
    sj = jnp.arange(n_sel)[None, :] * SEL_BLOCK
    overlap = ((ci < sj + SEL_BLOCK) & (ci + CMP_LEN > sj)).astype(jnp.float32)
    ks = apply_rope(k_slc[:, :, None, :], cos, sin)[:, :, 0].reshape(B, n_sel, SEL_BLOCK, HEAD_DIM)
    vs = v_slc.reshape(B, n_sel, SEL_BLOCK, HEAD_DIM)
    sel_j = jnp.arange(n_sel)
    nq = S // NSA_QBLK
    qc_b = q.reshape(B, nq, NSA_QBLK, NSA_HEADS, HEAD_DIM).transpose(1, 0, 2, 3, 4)
    qs_b = q_r.reshape(B, nq, NSA_QBLK, NSA_HEADS, HEAD_DIM).transpose(1, 0, 2, 3, 4)
    t_b = jnp.arange(S).reshape(nq, NSA_QBLK)

    def block(args):
        qc, qs, t = args
        sc = jnp.einsum('bqhd,bjd->bqhj', qc, kc).astype(jnp.float32) * scale
        cval = cmp_end[None, :] <= t[:, None]
        anyv = jnp.any(cval, axis=-1).astype(jnp.float32)
        sc = jnp.where(cval[None, :, None, :], sc, NEG_INF)
        pc = jax.nn.softmax(sc, axis=-1) * anyv[None, :, None, None]
        o_cmp = jnp.einsum('bqhj,bjd->bqhd', pc.astype(vc.dtype), vc)
        imp = jnp.einsum('bqhj,jn->bqn', pc, overlap)
        blk_t = t // SEL_BLOCK
        forced = (sel_j[None, :] == 0) | (sel_j[None, :] == blk_t[:, None]) | (sel_j[None, :] == blk_t[:, None] - 1)
        causal_blk = sel_j[None, :] <= blk_t[:, None]
        score = jnp.where(forced, SEL_FORCE, jnp.where(causal_blk, imp, -SEL_FORCE))
        _, idx = lax.top_k(score, top_n)
        kg = jax.vmap(lambda kb, ix: kb[ix])(ks, idx)
        vg = jax.vmap(lambda vb, ix: vb[ix])(vs, idx)
        ss = jnp.einsum('bqhd,bqnkd->bqhnk', qs, kg).astype(jnp.float32) * scale
        kpos = idx[..., None] * SEL_BLOCK + jnp.arange(SEL_BLOCK)
        valid = kpos <= t[None, :, None, None]
        ss = jnp.where(valid[:, :, None], ss, NEG_INF)
        ps = jax.nn.softmax(ss.reshape(ss.shape[:3] + (-1,)), axis=-1).reshape(ss.shape)
        o_slc = jnp.einsum('bqhnk,bqnkd->bqhd', ps.astype(vg.dtype), vg)
        return o_cmp, o_slc

    o_cmp, o_slc = lax.map(block, (qc_b, qs_b, t_b))
    o_cmp = o_cmp.transpose(1, 0, 2, 3, 4).reshape(B, S, NSA_HEADS, HEAD_DIM)
    o_slc = o_slc.transpose(1, 0, 2, 3, 4).reshape(B, S, NSA_HEADS, HEAD_DIM)
    kw = apply_rope(k_win.reshape(B, S, NSA_KV, HEAD_DIM), cos, sin)
    o_win = banded_attention(q_r, kw, v_win.reshape(B, S, NSA_KV, HEAD_DIM), NSA_WINDOW)
    g = jax.nn.sigmoid(gate_logit.astype(jnp.float32)).reshape(B, S, NSA_HEADS, 3)
    o = g[..., 0:1] * o_cmp + g[..., 1:2] * o_slc + g[..., 2:3] * o_win
    return o.reshape(B, S, NSA_WIDTH).astype(dt)


def hgrn2_mixer(q, z_f, i_in, g, lb, norm_g):
    B, S, _ = q.shape
    H, Dh, C = HGRN_HEADS, HGRN_DIM, HGRN_CHUNK
    f32 = jnp.float32
    lb = lb.astype(f32).reshape(H, Dh)
    z = z_f.astype(f32).reshape(B, S, H, Dh)
    log_f = jnp.logaddexp(jnp.log(jnp.maximum(lb, LOG_FLOOR)), jnp.log1p(-lb) + jax.nn.log_sigmoid(z))
    k = (1.0 - lb) * jax.nn.sigmoid(-z)
    nC = S // C

    def chunks(t):
        return t.astype(f32).reshape(B, nC, C, H, Dh).transpose(1, 0, 3, 2, 4)

    qc = chunks(q.reshape(B, S, H, Dh))
    kc = chunks(k)
    vc = chunks(i_in.reshape(B, S, H, Dh))
    lfc = chunks(log_f)
    tri = jnp.tril(jnp.ones((C, C), dtype=bool))[None, None, :, :, None]

    def step(state, inp):
        qt, kt, vt, lf = inp
        b = jnp.cumsum(lf, axis=2)
        diff = b[:, :, :, None, :] - b[:, :, None, :, :]
        decay = jnp.where(tri, jnp.exp(jnp.where(tri, diff, 0.0)), 0.0)
        att = jnp.einsum('bhtc,bhsc,bhtsc->bhts', qt, kt, decay)
        o = jnp.einsum('bhts,bhsv->bhtv', att, vt) + jnp.einsum('bhtc,bhcv->bhtv', qt * jnp.exp(b), state)
        b_last = b[:, :, -1:, :]
        new_state = state * jnp.exp(b_last[:, :, 0, :, None]) + jnp.einsum('bhsc,bhsv->bhcv', kt * jnp.exp(b_last - b), vt)
        return new_state, o

    state0 = jnp.zeros((B, H, Dh, Dh), f32)
    _, o = lax.scan(step, state0, (qc, kc, vc, lfc))
    o = o.transpose(1, 0, 3, 2, 4).reshape(B, S, H, Dh)
    o = o * lax.rsqrt(jnp.mean(o * o, axis=-1, keepdims=True) + RMS_EPS)
    o = o.reshape(B, S, HGRN_WIDTH) * norm_g.astype(f32) * jax.nn.silu(g.astype(f32))
    return o.astype(q.dtype)


def token_mix(x, w_in, w_out, cmp_pos, wk1, wk2, wv1, wv2, sinks, lb, norm_g, cos, sin):
    B, S, _ = x.shape
    kvd = NSA_KV * HEAD_DIM
    skv = SWA_KV * HEAD_DIM
    sizes = [NSA_WIDTH, kvd, kvd, kvd, kvd, kvd, kvd, 3 * NSA_HEADS,
             SWA_WIDTH, skv, skv,
             HGRN_WIDTH, HGRN_WIDTH, HGRN_WIDTH, HGRN_WIDTH]
    offs = np.cumsum(sizes)[:-1].tolist()
    (nq, kcm, vcm, ksl, vsl, kwn, vwn, ngl, sq, sk, sv, hq, hf, hi, hg) = jnp.split(x @ w_in, offs, axis=-1)
    o_nsa = nsa_mixer(nq, kcm, vcm, ksl, vsl, kwn, vwn, ngl, cmp_pos, wk1, wk2, wv1, wv2, cos, sin)
    qs = apply_rope(sq.reshape(B, S, SWA_HEADS, HEAD_DIM), cos, sin)
    ksw = apply_rope(sk.reshape(B, S, SWA_KV, HEAD_DIM), cos, sin)
    o_swa = banded_attention(qs, ksw, sv.reshape(B, S, SWA_KV, HEAD_DIM), SWA_WINDOW, sinks).reshape(B, S, SWA_WIDTH)
    o_hgrn = hgrn2_mixer(hq, hf, hi, hg, lb, norm_g)
    return jnp.concatenate([o_nsa, o_swa, o_hgrn], axis=-1) @ w_out


def setup_inputs(seed: int = 0) -> dict:
    key = jax.random.key(seed)
    ks = jax.random.split(key, 20)
    f32 = jnp.float32
    beta = (8.0 * DEPTH) ** -0.25

    def nrm(k, shape, scale):
        return jax.random.normal(k, shape, f32) * scale

    return {
        'x': nrm(ks[0], (BATCH, SEQ, D_MODEL), 1.0),
        'p': nrm(ks[1], (DEPTH, BATCH, SEQ, PLE_DIM), 1.0),
        'ln_g': 1.0 + nrm(ks[2], (DEPTH, 3, D_MODEL), 0.02),
        'ln_b': nrm(ks[3], (DEPTH, 3, D_MODEL), 0.02),
        'ffn_w_gu': nrm(ks[4], (DEPTH, 2, D_MODEL, 2 * D_FF), D_MODEL ** -0.5),
        'ffn_w_down': nrm(ks[5], (DEPTH, 2, D_FF, D_MODEL), beta * D_FF ** -0.5),
        'w_in': nrm(ks[6], (DEPTH, D_MODEL, N_IN), D_MODEL ** -0.5),
        'w_out': nrm(ks[7], (DEPTH, D_MIX, D_MODEL), beta * D_MIX ** -0.5),
        'cmp_pos': nrm(ks[8], (DEPTH, CMP_LEN, HEAD_DIM), 0.1),
        'cmp_k_w1': nrm(ks[9], (DEPTH, CMP_LEN * HEAD_DIM, CMP_HID), (CMP_LEN * HEAD_DIM) ** -0.5),
        'cmp_k_w2': nrm(ks[10], (DEPTH, CMP_HID, HEAD_DIM), CMP_HID ** -0.5),
        'cmp_v_w1': nrm(ks[11], (DEPTH, CMP_LEN * HEAD_DIM, CMP_HID), (CMP_LEN * HEAD_DIM) ** -0.5),
        'cmp_v_w2': nrm(ks[12], (DEPTH, CMP_HID, HEAD_DIM), CMP_HID ** -0.5),
        'swa_sinks': nrm(ks[13], (DEPTH, SWA_HEADS), 0.5),
        'hgrn_lb_raw': nrm(ks[14], (DEPTH, HGRN_WIDTH), 0.1),
        'hgrn_norm_g': 1.0 + nrm(ks[15], (DEPTH, HGRN_WIDTH), 0.02),
        'ple_w': nrm(ks[16], (DEPTH, PLE_DIM, D_MODEL), beta * PLE_DIM ** -0.5),
        'ple_gate_w': nrm(ks[17], (DEPTH, D_MODEL, D_MODEL), D_MODEL ** -0.5),
        'ple_gate_b': nrm(ks[18], (DEPTH, D_MODEL), 0.02),
    }


def reference(x, p, ln_g, ln_b, ffn_w_gu, ffn_w_down, w_in, w_out, cmp_pos,
              cmp_k_w1, cmp_k_w2, cmp_v_w1, cmp_v_w2, swa_sinks, hgrn_lb_raw,
              hgrn_norm_g, ple_w, ple_gate_w, ple_gate_b):
    B, S, _ = x.shape
    cos, sin = rope_tables(S)
    alpha = (2.0 * DEPTH) ** 0.25
    sm = jax.nn.softmax(hgrn_lb_raw.astype(jnp.float32), axis=0)
    lower_bounds = jnp.cumsum(sm, axis=0) - sm[0]
    for i in range(DEPTH):
        x = layer_norm(alpha * x + 0.5 * swiglu(x, ffn_w_gu[i, 0], ffn_w_down[i, 0]), ln_g[i, 0], ln_b[i, 0])
        y = token_mix(x, w_in[i], w_out[i], cmp_pos[i], cmp_k_w1[i], cmp_k_w2[i], cmp_v_w1[i], cmp_v_w2[i],
                      swa_sinks[i], lower_bounds[i], hgrn_norm_g[i], cos, sin)
        x = layer_norm(alpha * x + y, ln_g[i, 1], ln_b[i, 1])
        x = layer_norm(alpha * x + 0.5 * swiglu(x, ffn_w_gu[i, 1], ffn_w_down[i, 1]), ln_g[i, 2], ln_b[i, 2])
        x = x + jax.nn.sigmoid(x @ ple_gate_w[i] + ple_gate_b[i]) * (p[i] @ ple_w[i])
    return x
```

```python
import functools

import numpy as np
import jax
import jax.numpy as jnp
from jax import lax
from jax.experimental import pallas as pl
from jax.experimental.pallas import tpu as pltpu

F32 = jnp.float32
BF16 = jnp.bfloat16

HEAD_DIM = 64
ROPE_THETA = 10000.0
LN_EPS = 1e-5
RMS_EPS = 1e-6
NEG_INF = -1e30
LOG_FLOOR = 1e-30
NSA_HEADS = 4
CMP_LEN = 32
CMP_STRIDE = 16
SEL_BLOCK = 64
SEL_TOPN = 16
SEL_FORCE = 1e4
NSA_WINDOW = 512
SWA_HEADS = 4
SWA_KV = 2
SWA_WINDOW = 128
HGRN_HEADS = 4
HGRN_DIM = 128
HGRN_CHUNK = 64

LANES = 128
VMEM_LIMIT = 56 * 1024 * 1024
MASK_BIG = 2.0 ** 100
ATTN_TQ = 128
SEL_CK = 512

NT_DIMS = (((1,), (1,)), ((), ()))
TN_DIMS = (((0,), (0,)), ((), ()))


def _dot(a, b):
    return jnp.dot(a, b, preferred_element_type=F32)


def _dot_nt(a, b):
    return lax.dot_general(a, b, NT_DIMS, preferred_element_type=F32)


def _dot_tn(a, b):
    return lax.dot_general(a, b, TN_DIMS, preferred_element_type=F32)


def _split3(x):
    hi = x.astype(BF16)
    r1 = x - hi.astype(F32)
    mid = r1.astype(BF16)
    lo = (r1 - mid.astype(F32)).astype(BF16)
    return hi, mid, lo


def _layer_norm(y, g, b):
    mu = jnp.mean(y, axis=-1, keepdims=True)
    d = y - mu
    var = jnp.mean(d * d, axis=-1, keepdims=True)
    return d * lax.rsqrt(var + LN_EPS) * g + b


def _resident(shape):
    nd = len(shape)
    return pl.BlockSpec(shape, lambda *_: (0,) * nd, pipeline_mode=pl.Buffered(1))


def _params(sem):
    return pltpu.CompilerParams(dimension_semantics=sem, vmem_limit_bytes=VMEM_LIMIT)


def _ffn_body(*refs, d_ff, chunk, alpha, ple):
    if ple:
        x_ref, wgu_ref, wd_ref, g_ref, b_ref, p_ref, pw_ref, gw_ref, gb_ref, o_ref = refs
    else:
        x_ref, wgu_ref, wd_ref, g_ref, b_ref, o_ref = refs
    x = x_ref[...]
    xb = x.astype(BF16)
    acc = jnp.zeros(x.shape, F32)
    for j in range(d_ff // chunk):
        a = _dot(xb, wgu_ref[:, j * chunk:(j + 1) * chunk])
        u = _dot(xb, wgu_ref[:, d_ff + j * chunk:d_ff + (j + 1) * chunk])
        h = (a * jax.nn.sigmoid(a) * u).astype(BF16)
        acc = acc + _dot(h, wd_ref[j * chunk:(j + 1) * chunk, :])
    y = _layer_norm(alpha * x + 0.5 * acc, g_ref[...], b_ref[...])
    if ple:
        gate = jax.nn.sigmoid(_dot(y.astype(BF16), gw_ref[...]) + gb_ref[...])
        y = y + gate * _dot(p_ref[...].astype(BF16), pw_ref[...])
    o_ref[...] = y


def _ffn_ln(x, wgu, wd, g, b, alpha, ple=None, tm=512, chunk=256):
    T, D = x.shape
    d_ff = wd.shape[0]
    row = lambda w: pl.BlockSpec((tm, w), lambda i: (i, 0))
    in_specs = [row(D), _resident(wgu.shape), _resident(wd.shape), _resident((1, D)), _resident((1, D))]
    args = [x, wgu, wd, g, b]
    if ple is not None:
        p, pw, gw, gb = ple
        in_specs += [row(p.shape[1]), _resident(pw.shape), _resident(gw.shape), _resident((1, D))]
        args += [p, pw, gw, gb]
    return pl.pallas_call(
        functools.partial(_ffn_body, d_ff=d_ff, chunk=chunk, alpha=alpha, ple=ple is not None),
        grid=(T // tm,),
        in_specs=in_specs,
        out_specs=row(D),
        out_shape=jax.ShapeDtypeStruct((T, D), F32),
        compiler_params=_params(("parallel",)),
        name="ffn_ln_ple" if ple is not None else "ffn_ln",
    )(*args)


A_NSA_Q, A_NSA_QR, A_SWA_QR = 0, 256, 512
A_KA, A_KB, A_VS, A_KW, A_VW = 768, 896, 1024, 1152, 1280
A_SK0, A_SK1, A_SV0, A_SV1 = 1408, 1536, 1664, 1792
A_WIDTH = 1920
P_KC, P_VC, P_GATE, P_HGRN = 1920, 2048, 2176, 2304
P_WIDTH = 2304 + 2048


def _rope(x, cosf, sinf, first_half):
    rot = jnp.where(first_half, pltpu.roll(x, 96, 1), pltpu.roll(x, 32, 1))
    return x * cosf + rot * sinf


def _proj_body(x_ref, w_ref, cos_ref, sin_ref, a_ref, h_ref, kc_ref, vc_ref, gate_ref, *, tm, seq, scale):
    xb = x_ref[...].astype(BF16)
    cosf = cos_ref[...]
    sinf = sin_ref[...]
    lane = lax.broadcasted_iota(jnp.int32, (tm, LANES), 1)
    first_half = (lane % HEAD_DIM) < (HEAD_DIM // 2)
    low = lane < HEAD_DIM
    pos = (pl.program_id(0) % (seq // tm)) * tm + lax.broadcasted_iota(jnp.int32, (tm, LANES), 0)
    blk = pos // SEL_BLOCK

    def mm(c0, width=LANES):
        return _dot(xb, w_ref[:, c0:c0 + width])

    def put(c0, val):
        a_ref[:, c0:c0 + LANES] = val.astype(BF16)

    for half in range(2):
        off = half * LANES
        put(A_NSA_Q + off, mm(A_NSA_Q + off) * scale)
        put(A_NSA_QR + off, _rope(mm(A_NSA_QR + off), cosf, sinf, first_half) * scale)
        put(A_SWA_QR + off, _rope(mm(A_SWA_QR + off), cosf, sinf, first_half) * scale)
    ka = _rope(mm(A_KA), cosf, sinf, first_half)
    put(A_KA, jnp.where(low, ka, jnp.where(lane - HEAD_DIM == blk, 1.0, 0.0)))
    kb = _rope(mm(A_KB), cosf, sinf, first_half)
    put(A_KB, jnp.where(low, jnp.where(lane == blk, 1.0, 0.0), kb))
    for c0 in (A_KW, A_SK0, A_SK1):
        put(c0, _rope(mm(c0), cosf, sinf, first_half))
    for c0 in (A_VS, A_VW, A_SV0, A_SV1):
        put(c0, mm(c0))
    kc_ref[...] = mm(P_KC)[:, :HEAD_DIM]
    vc_ref[...] = mm(P_VC)[:, :HEAD_DIM]
    gate_ref[...] = jax.nn.sigmoid(mm(P_GATE))
    for j in range(4):
        h_ref[:, j * 512:(j + 1) * 512] = mm(P_HGRN + j * 512, 512)


def _proj_in(x, w, cosf, sinf, seq, tm=512):
    T, D = x.shape
    nseq = seq // tm
    row = lambda w_: pl.BlockSpec((tm, w_), lambda i: (i, 0))
    tab = pl.BlockSpec((tm, LANES), lambda i: (i % nseq, 0))
    return pl.pallas_call(
        functools.partial(_proj_body, tm=tm, seq=seq, scale=HEAD_DIM ** -0.5),
        grid=(T // tm,),
        in_specs=[row(D), _resident(w.shape), tab, tab],
        out_specs=[row(A_WIDTH), row(2048), row(HEAD_DIM), row(HEAD_DIM), row(LANES)],
        out_shape=[jax.ShapeDtypeStruct((T, A_WIDTH), BF16), jax.ShapeDtypeStruct((T, 2048), F32),
                   jax.ShapeDtypeStruct((T, HEAD_DIM), F32), jax.ShapeDtypeStruct((T, HEAD_DIM), F32),
                   jax.ShapeDtypeStruct((T, LANES), F32)],
        compiler_params=_params(("parallel",)),
        name="proj_in",
    )(x, w, cosf, sinf)


def _pack_w_in(w_in):
    offs = np.cumsum([0, 256, 64, 64, 64, 64, 64, 64, 12, 256, 128, 128, 512, 512, 512, 512])
    (nq, kcm, vcm, ksl, vsl, kwn, vwn, ngl, sq, sk, sv, hq, hf, hi, hg) = [
        w_in[:, offs[i]:offs[i + 1]] for i in range(15)]
    z = lambda n: jnp.zeros((w_in.shape[0], n), w_in.dtype)
    dup = lambda t: jnp.concatenate([t, t], axis=1)
    cols = [nq, nq, sq, dup(ksl), dup(ksl), dup(vsl), dup(kwn), dup(vwn),
            dup(sk[:, :64]), dup(sk[:, 64:]), dup(sv[:, :64]), dup(sv[:, 64:]),
            kcm, z(64), vcm, z(64), ngl, z(LANES - 12), hq, hf, hi, hg]
    w = jnp.concatenate(cols, axis=1).astype(BF16)
    assert w.shape[1] == P_WIDTH
    return w


def _cmp_body(ak_ref, av_ref, pos_ref, wk1_ref, wk2_ref, wv1_ref, wv2_ref, kc_ref, vc_ref):
    half = pos_ref.shape[1]
    for a_ref, w1_ref, w2_ref, o_ref in ((ak_ref, wk1_ref, wk2_ref, kc_ref), (av_ref, wv1_ref, wv2_ref, vc_ref)):
        a = a_ref[0]
        n = a.shape[0]
        p_lo = _dot((a + pos_ref[0:1, :]).astype(BF16), w1_ref[0:half, :])
        p_hi = _dot((a + pos_ref[1:2, :]).astype(BF16), w1_ref[half:2 * half, :])
        pre = p_lo + pltpu.roll(p_hi, n - 1, 0)
        act = (pre * jax.nn.sigmoid(pre)).astype(BF16)
        o_ref[0] = _dot(act, w2_ref[...]).astype(BF16)


def _compress(ak, av, pos2, wk1, wk2, wv1, wv2):
    B, n, w = ak.shape
    blk = pl.BlockSpec((1, n, w), lambda b: (b, 0, 0))
    out = pl.BlockSpec((1, n, LANES), lambda b: (b, 0, 0))
    return pl.pallas_call(
        _cmp_body,
        grid=(B,),
        in_specs=[blk, blk, _resident(pos2.shape), _resident(wk1.shape), _resident(wk2.shape),
                  _resident(wv1.shape), _resident(wv2.shape)],
        out_specs=[out, out],
        out_shape=[jax.ShapeDtypeStruct((B, n, LANES), BF16)] * 2,
        compiler_params=_params(("parallel",)),
        name="nsa_compress",
    )(ak, av, pos2, wk1, wk2, wv1, wv2)


def _stack_heads(q2, fill, low):
    c0, c1 = q2[:, :LANES], q2[:, LANES:]
    even = jnp.concatenate([jnp.where(low, c0, fill), jnp.where(low, c1, fill)], axis=0)
    odd = jnp.concatenate([jnp.where(low, fill, c0), jnp.where(low, fill, c1)], axis=0)
    return even, odd


def _nsa_body(q_ref, qr_ref, gate_ref, kc_ref, vc_ref, ovl_ref, ka_ref, kb_ref, vs_ref, kw_ref, vw_ref, o_ref,
              *, seq, top_n):
    TQ = ATTN_TQ
    n_sel = seq // SEL_BLOCK
    n_cmp = kc_ref.shape[1]
    start = pl.program_id(1) * TQ
    lane = lax.broadcasted_iota(jnp.int32, (TQ, LANES), 1)
    low = lane < HEAD_DIM
    zero = jnp.zeros((TQ, LANES), BF16)
    t1 = start + lax.broadcasted_iota(jnp.int32, (TQ, 1), 0)
    t4 = jnp.concatenate([t1, t1, t1, t1], axis=0)

    qe, qo = _stack_heads(q_ref[...], zero, low)
    qc = jnp.concatenate([qe, qo], axis=0)
    sc = _dot_nt(qc, kc_ref[0])
    cend = lax.broadcasted_iota(jnp.int32, (1, n_cmp), 1) * CMP_STRIDE + (CMP_LEN - 1)
    sc = jnp.where(cend <= t4, sc, NEG_INF)
    e = jnp.exp(sc - jnp.max(sc, axis=-1, keepdims=True))
    pc = e / jnp.sum(e, axis=-1, keepdims=True)
    pc = jnp.where(t4 >= CMP_LEN - 1, pc, 0.0)
    o_cmp = _dot(pc.astype(BF16), vc_ref[0])
    pcs = pc[0:TQ] + pc[TQ:2 * TQ] + pc[2 * TQ:3 * TQ] + pc[3 * TQ:4 * TQ]
    ovl = ovl_ref[...]
    imp_t = sum(_dot_nt(ovl, piece) for piece in _split3(pcs))

    j = lax.broadcasted_iota(jnp.int32, (n_sel, TQ), 0)
    blk_t = (start + lax.broadcasted_iota(jnp.int32, (n_sel, TQ), 1)) // SEL_BLOCK
    forced = (j == 0) | (j == blk_t) | (j == blk_t - 1)
    score = jnp.where(forced, SEL_FORCE, jnp.where(j <= blk_t, imp_t, -SEL_FORCE))
    rank = jnp.zeros((n_sel, TQ), F32)
    for jp in range(n_sel):
        row = score[jp:jp + 1, :]
        beats = (row > score) | ((row == score) & (j > jp))
        rank = rank + jnp.where(beats, 1.0, 0.0)
    bias_t = jnp.where(rank < top_n, 0.0, -MASK_BIG)
    mdup = jnp.concatenate([bias_t] * (LANES // n_sel), axis=0).T.astype(BF16)

    qr = qr_ref[...]
    qse, qso = _stack_heads(qr, mdup, low)

    def sel_step(c, carry):
        m, l, acc = carry
        ks = pl.multiple_of(c * SEL_CK, SEL_CK)
        s = jnp.concatenate([_dot_nt(qse, ka_ref[pl.ds(ks, SEL_CK), :]),
                             _dot_nt(qso, kb_ref[pl.ds(ks, SEL_CK), :])], axis=0)
        kpos = ks + lax.broadcasted_iota(jnp.int32, (1, SEL_CK), 1)
        s = jnp.where(kpos <= t4, s, NEG_INF)
        m_new = jnp.maximum(m, jnp.max(s, axis=-1, keepdims=True))
        a = jnp.exp(m - m_new)
        p = jnp.exp(s - m_new)
        l = a * l + jnp.sum(p, axis=-1, keepdims=True)
        acc = a * acc + _dot(p.astype(BF16), vs_ref[pl.ds(ks, SEL_CK), :])
        return m_new, l, acc

    n_chunks = (start + TQ + SEL_CK - 1) // SEL_CK
    init = (jnp.full((4 * TQ, 1), NEG_INF, F32), jnp.zeros((4 * TQ, 1), F32), jnp.zeros((4 * TQ, LANES), F32))
    _, l_s, acc_s = lax.fori_loop(0, n_chunks, sel_step, init)
    o_slc = acc_s / l_s

    span = NSA_WINDOW + TQ
    kstart = pl.multiple_of(jnp.maximum(start - NSA_WINDOW, 0), TQ)
    qwe, qwo = _stack_heads(qr, zero, low)
    sw = _dot_nt(jnp.concatenate([qwe, qwo], axis=0), kw_ref[pl.ds(kstart, span), :])
    diff = t4 - (kstart + lax.broadcasted_iota(jnp.int32, (1, span), 1))
    sw = jnp.where((diff >= 0) & (diff < NSA_WINDOW), sw, NEG_INF)
    pw = jnp.exp(sw - jnp.max(sw, axis=-1, keepdims=True))
    o_win = _dot(pw.astype(BF16), vw_ref[pl.ds(kstart, span), :]) / jnp.sum(pw, axis=-1, keepdims=True)

    g = gate_ref[...]
    heads = []
    for h, r in ((0, 0), (1, 2), (2, 1), (3, 3)):
        rows = slice(r * TQ, (r + 1) * TQ)
        heads.append(g[:, 3 * h:3 * h + 1] * o_cmp[rows] + g[:, 3 * h + 1:3 * h + 2] * o_slc[rows]
                     + g[:, 3 * h + 2:3 * h + 3] * o_win[rows])
    o_ref[...] = jnp.concatenate([jnp.where(low, heads[0], heads[1]),
                                  jnp.where(low, heads[2], heads[3])], axis=1).astype(o_ref.dtype)


def _nsa_attention(a, gates, kc, vc, ovl, batch, seq):
    T = a.shape[0]
    nq = seq // ATTN_TQ
    n_cmp = kc.shape[1]
    n_sel = seq // SEL_BLOCK
    assert LANES % n_sel == 0 and n_sel <= HEAD_DIM and seq >= NSA_WINDOW + ATTN_TQ and seq % SEL_CK == 0
    qspec = lambda cb: pl.BlockSpec((ATTN_TQ, 256), lambda b, i: (b * nq + i, cb))
    kspec = lambda c0: pl.BlockSpec((seq, LANES), lambda b, i: (b, c0 // LANES))
    cspec = pl.BlockSpec((1, n_cmp, LANES), lambda b, i: (b, 0, 0))
    return pl.pallas_call(
        functools.partial(_nsa_body, seq=seq, top_n=min(SEL_TOPN, n_sel)),
        grid=(batch, nq),
        in_specs=[qspec(A_NSA_Q // 256), qspec(A_NSA_QR // 256),
                  pl.BlockSpec((ATTN_TQ, LANES), lambda b, i: (b * nq + i, 0)),
                  cspec, cspec, pl.BlockSpec(ovl.shape, lambda b, i: (0, 0)),
                  kspec(A_KA), kspec(A_KB), kspec(A_VS), kspec(A_KW), kspec(A_VW)],
        out_specs=pl.BlockSpec((ATTN_TQ, 256), lambda b, i: (b * nq + i, 0)),
        out_shape=jax.ShapeDtypeStruct((T, 256), BF16),
        compiler_params=_params(("parallel", "arbitrary")),
        name="nsa_attention",
    )(a, a, gates, kc, vc, ovl, a, a, a, a, a)


def _overlap_t(seq):
    n_pad = seq // CMP_STRIDE
    n_cmp = (seq - CMP_LEN) // CMP_STRIDE + 1
    ci = np.arange(n_pad)[None, :] * CMP_STRIDE
    sj = np.arange(seq // SEL_BLOCK)[:, None] * SEL_BLOCK
    ovl = (ci < sj + SEL_BLOCK) & (ci + CMP_LEN > sj) & (np.arange(n_pad)[None, :] < n_cmp)
    return jnp.asarray(ovl.astype(np.float32), dtype=BF16)


def _swa_body(sink_ref, q_ref, k0_ref, k1_ref, v0_ref, v1_ref, o_ref):
    TQ = ATTN_TQ
    start = pl.program_id(1) * TQ
    span = SWA_WINDOW + TQ
    kstart = pl.multiple_of(jnp.maximum(start - SWA_WINDOW, 0), TQ)
    lane = lax.broadcasted_iota(jnp.int32, (TQ, LANES), 1)
    low = lane < HEAD_DIM
    zero = jnp.zeros((TQ, LANES), BF16)
    t1 = start + lax.broadcasted_iota(jnp.int32, (TQ, 1), 0)
    t2 = jnp.concatenate([t1, t1], axis=0)
    diff = t2 - (kstart + lax.broadcasted_iota(jnp.int32, (1, span), 1))
    mask = (diff >= 0) & (diff < SWA_WINDOW)
    q = q_ref[...]
    cols = []
    for kv, (k_ref, v_ref) in enumerate(((k0_ref, v0_ref), (k1_ref, v1_ref))):
        c = q[:, kv * LANES:(kv + 1) * LANES]
        qs = jnp.concatenate([jnp.where(low, c, zero), jnp.where(low, zero, c)], axis=0)
        s = jnp.where(mask, _dot_nt(qs, k_ref[pl.ds(kstart, span), :]), NEG_INF)
        sink = jnp.concatenate([jnp.full((TQ, 1), sink_ref[2 * kv], F32),
                                jnp.full((TQ, 1), sink_ref[2 * kv + 1], F32)], axis=0)
        m = jnp.maximum(jnp.max(s, axis=-1, keepdims=True), sink)
        p = jnp.exp(s - m)
        den = jnp.sum(p, axis=-1, keepdims=True) + jnp.exp(sink - m)
        o = _dot(p.astype(BF16), v_ref[pl.ds(kstart, span), :]) / den
        cols.append(jnp.where(low, o[:TQ], o[TQ:]))
    o_ref[...] = jnp.concatenate(cols, axis=1).astype(o_ref.dtype)


def _swa_attention(a, sinks, batch, seq):
    T = a.shape[0]
    nq = seq // ATTN_TQ
    assert seq >= SWA_WINDOW + ATTN_TQ
    kspec = lambda c0: pl.BlockSpec((seq, LANES), lambda b, i: (b, c0 // LANES))
    return pl.pallas_call(
        _swa_body,
        grid=(batch, nq),
        in_specs=[pl.BlockSpec(memory_space=pltpu.SMEM),
                  pl.BlockSpec((ATTN_TQ, 256), lambda b, i: (b * nq + i, A_SWA_QR // 256)),
                  kspec(A_SK0), kspec(A_SK1), kspec(A_SV0), kspec(A_SV1)],
        out_specs=pl.BlockSpec((ATTN_TQ, 256), lambda b, i: (b * nq + i, 0)),
        out_shape=jax.ShapeDtypeStruct((T, 256), BF16),
        compiler_params=_params(("parallel", "arbitrary")),
        name="swa_attention",
    )(sinks, a, a, a, a, a)


HGRN_LEVELS = (32, 16, 8, 4, 2, 1)


def _hgrn_tables():
    C = HGRN_CHUNK
    t = np.arange(C)[:, None]
    u = np.arange(C)[None, :]
    mats, masks = [], []
    for m in HGRN_LEVELS:
        r = (t // (2 * m)) * 2 * m + m - 1
        mats.append(np.where(t > r, (u > r) & (u <= t), (u > t) & (u <= r)))
        masks.append((t // (2 * m)) == (u // (2 * m)))
    mats.append(u <= t)
    mats.append(u > t)
    masks.append(t == u)
    return (jnp.asarray(np.concatenate(mats, 0).astype(np.float32), dtype=BF16),
            jnp.asarray(np.stack(masks).astype(np.float32)))


def _hgrn_body(q_ref, f_ref, i_ref, g_ref, lbraw_ref, ng_ref, amat_ref, pmask_ref, o_ref, state_ref,
               *, layer, rows):
    C, Dh = HGRN_CHUNK, HGRN_DIM
    nlev = len(HGRN_LEVELS)

    @pl.when(pl.program_id(1) == 0)
    def _():
        state_ref[...] = jnp.zeros(state_ref.shape, F32)

    raw = lbraw_ref[...]
    ex = jnp.exp(raw - jnp.max(raw, axis=0, keepdims=True))
    sm = ex / jnp.sum(ex, axis=0, keepdims=True)
    cs = sm[0:1]
    for k in range(1, layer + 1):
        cs = cs + sm[k:k + 1]
    lb = cs - sm[0:1]
    log_lb = jnp.log(jnp.maximum(lb, LOG_FLOOR))
    log_1m = jnp.log1p(-lb)
    row = lax.broadcasted_iota(jnp.int32, (C, Dh), 0)
    amat = amat_ref[...]

    def chunk(ci, carry):
        r0 = pl.multiple_of(ci * C, C)
        for h in range(HGRN_HEADS):
            cols = slice(h * Dh, (h + 1) * Dh)
            z = f_ref[pl.ds(r0, C), cols]
            q = q_ref[pl.ds(r0, C), cols]
            v = i_ref[pl.ds(r0, C), cols].astype(BF16)
            c = log_1m[:, cols] + (jnp.minimum(z, 0.0) - jnp.log1p(jnp.exp(-jnp.abs(z))))
            a = log_lb[:, cols]
            lf = jnp.maximum(a, c) + jnp.log1p(jnp.exp(-jnp.abs(a - c)))
            k = (1.0 - lb[:, cols]) * jax.nn.sigmoid(-z)
            d3 = _dot(amat, jnp.concatenate(_split3(lf), axis=1))
            dsum = d3[:, :Dh] + d3[:, Dh:2 * Dh] + d3[:, 2 * Dh:]
            att = pmask_ref[nlev] * _dot_nt(q.astype(BF16), k.astype(BF16))
            for lev, m in enumerate(HGRN_LEVELS):
                dec = jnp.exp(dsum[lev * C:(lev + 1) * C])
                upper = (row // m) % 2 == 1
                qt = (jnp.where(upper, q, 0.0) * dec).astype(BF16)
                kt = (jnp.where(upper, 0.0, k) * dec).astype(BF16)
                att = att + pmask_ref[lev] * _dot_nt(qt, kt)
            b = dsum[nlev * C:(nlev + 1) * C]
            rb = dsum[(nlev + 1) * C:(nlev + 2) * C]
            st = state_ref[h]
            o = _dot(att.astype(BF16), v) + _dot_nt((q * jnp.exp(b)).astype(BF16), st.astype(BF16))
            state_ref[h] = st * jnp.exp(b[C - 1:C, :]) + _dot_tn(v, (k * jnp.exp(rb)).astype(BF16))
            o = o * lax.rsqrt(jnp.mean(o * o, axis=-1, keepdims=True) + RMS_EPS)
            gg = g_ref[pl.ds(r0, C), cols]
            o_ref[pl.ds(r0, C), cols] = (o * ng_ref[:, cols] * (gg * jax.nn.sigmoid(gg))).astype(o_ref.dtype)
        return carry

    lax.fori_loop(0, rows // C, chunk, 0)


def _hgrn(hf, lb_raw, norm_g, amat, pmask, layer, batch, seq, rows=512):
    T = hf.shape[0]
    W = HGRN_HEADS * HGRN_DIM
    ns = seq // rows
    spec = lambda cb: pl.BlockSpec((rows, W), lambda b, i: (b * ns + i, cb))
    return pl.pallas_call(
        functools.partial(_hgrn_body, layer=layer, rows=rows),
        grid=(batch, ns),
        in_specs=[spec(0), spec(1), spec(2), spec(3),
                  pl.BlockSpec(lb_raw.shape, lambda b, i: (0, 0)), pl.BlockSpec((1, W), lambda b, i: (0, 0)),
                  pl.BlockSpec(amat.shape, lambda b, i: (0, 0)), pl.BlockSpec(pmask.shape, lambda b, i: (0, 0, 0))],
        out_specs=pl.BlockSpec((rows, W), lambda b, i: (b * ns + i, 0)),
        out_shape=jax.ShapeDtypeStruct((T, W), BF16),
        scratch_shapes=[pltpu.VMEM((HGRN_HEADS, HGRN_DIM, HGRN_DIM), F32)],
        compiler_params=_params(("parallel", "arbitrary")),
        name="hgrn2",
    )(hf, hf, hf, hf, lb_raw, norm_g, amat, pmask)


def _out_body(x_ref, on_ref, os_ref, oh_ref, w_ref, g_ref, b_ref, o_ref, *, alpha):
    mix = jnp.concatenate([on_ref[...], os_ref[...], oh_ref[...]], axis=1)
    y = alpha * x_ref[...] + _dot(mix, w_ref[...])
    o_ref[...] = _layer_norm(y, g_ref[...], b_ref[...])


def _out_proj_ln(x, o_nsa, o_swa, o_hgrn, w, g, b, alpha, tm=512):
    T, D = x.shape
    row = lambda w_: pl.BlockSpec((tm, w_), lambda i: (i, 0))
    return pl.pallas_call(
        functools.partial(_out_body, alpha=alpha),
        grid=(T // tm,),
        in_specs=[row(D), row(o_nsa.shape[1]), row(o_swa.shape[1]), row(o_hgrn.shape[1]),
                  _resident(w.shape), _resident((1, D)), _resident((1, D))],
        out_specs=row(D),
        out_shape=jax.ShapeDtypeStruct((T, D), F32),
        compiler_params=_params(("parallel",)),
        name="out_proj_ln",
    )(x, o_nsa, o_swa, o_hgrn, w, g, b)


def _rope_tables(seq):
    inv = 1.0 / (ROPE_THETA ** (jnp.arange(0, HEAD_DIM, 2, dtype=F32) / HEAD_DIM))
    ang = jnp.arange(seq, dtype=F32)[:, None] * inv[None, :]
    cos, sin = jnp.cos(ang), jnp.sin(ang)
    return jnp.tile(cos, (1, 4)), jnp.concatenate([-sin, sin, -sin, sin], axis=1)


def kernel(x, p, ln_g, ln_b, ffn_w_gu, ffn_w_down, w_in, w_out, cmp_pos, cmp_k_w1, cmp_k_w2, cmp_v_w1, cmp_v_w2,
           swa_sinks, hgrn_lb_raw, hgrn_norm_g, ple_w, ple_gate_w, ple_gate_b):
    B, S, D = x.shape
    depth = w_in.shape[0]
    T = B * S
    alpha = (2.0 * depth) ** 0.25
    cosf, sinf = _rope_tables(S)
    ovl = _overlap_t(S)
    amat, pmask = _hgrn_tables()
    dup = lambda w: jnp.concatenate([w, w], axis=1).astype(BF16)
    xt = x.reshape(T, D)
    for i in range(depth):
        vec = lambda v: v.reshape(1, -1)
        xt = _ffn_ln(xt, ffn_w_gu[i, 0].astype(BF16), ffn_w_down[i, 0].astype(BF16),
                     vec(ln_g[i, 0]), vec(ln_b[i, 0]), alpha)
        a, hf, kcm, vcm, gates = _proj_in(xt, _pack_w_in(w_in[i]), cosf, sinf, S)
        n16 = S // CMP_STRIDE
        kc, vc = _compress(kcm.reshape(B, n16, CMP_STRIDE * HEAD_DIM), vcm.reshape(B, n16, CMP_STRIDE * HEAD_DIM),
                           cmp_pos[i].reshape(2, CMP_STRIDE * HEAD_DIM),
                           cmp_k_w1[i].astype(BF16), dup(cmp_k_w2[i]), cmp_v_w1[i].astype(BF16), dup(cmp_v_w2[i]))
        o_nsa = _nsa_attention(a, gates, kc, vc, ovl, B, S)
        o_swa = _swa_attention(a, swa_sinks[i], B, S)
        o_hgrn = _hgrn(hf, hgrn_lb_raw, vec(hgrn_norm_g[i]), amat, pmask, i, B, S)
        xt = _out_proj_ln(xt, o_nsa, o_swa, o_hgrn, w_out[i].astype(BF16), vec(ln_g[i, 1]), vec(ln_b[i, 1]), alpha)
        xt = _ffn_ln(xt, ffn_w_gu[i, 1].astype(BF16), ffn_w_down[i, 1].astype(BF16),
                     vec(ln_g[i, 2]), vec(ln_b[i, 2]), alpha,
                     ple=(p[i].reshape(T, -1), ple_w[i].astype(BF16), ple_gate_w[i].astype(BF16),
                          vec(ple_gate_b[i])))
    return xt.reshape(B, S, D)
```

```python
import functools

import numpy as np
import jax
import jax.numpy as jnp
from jax import lax
from jax.experimental import pallas as pl
from jax.experimental.pallas import tpu as pltpu

F32 = jnp.float32
BF16 = jnp.bfloat16

HEAD_DIM = 64
ROPE_THETA = 10000.0
LN_EPS = 1e-5
RMS_EPS = 1e-6
NEG_INF = -1e30
LOG_FLOOR = 1e-30
NSA_HEADS = 4
CMP_LEN = 32
CMP_STRIDE = 16
SEL_BLOCK = 64
SEL_TOPN = 16
SEL_FORCE = 1e4
NSA_WINDOW = 512
SWA_HEADS = 4
SWA_KV = 2
SWA_WINDOW = 128
HGRN_HEADS = 4
HGRN_DIM = 128
HGRN_CHUNK = 64

LANES = 128
VMEM_LIMIT = 56 * 1024 * 1024
MASK_BIG = 2.0 ** 100
ATTN_TQ = 128
SEL_CK = 512

NT_DIMS = (((1,), (1,)), ((), ()))
TN_DIMS = (((0,), (0,)), ((), ()))


def _dot(a, b):
    return jnp.dot(a, b, preferred_element_type=F32)


def _dot_nt(a, b):
    return lax.dot_general(a, b, NT_DIMS, preferred_element_type=F32)


def _dot_tn(a, b):
    return lax.dot_general(a, b, TN_DIMS, preferred_element_type=F32)


def _split3(x):
    hi = x.astype(BF16)
    r1 = x - hi.astype(F32)
    mid = r1.astype(BF16)
    lo = (r1 - mid.astype(F32)).astype(BF16)
    return hi, mid, lo


def _layer_norm(y, g, b):
    mu = jnp.mean(y, axis=-1, keepdims=True)
    d = y - mu
    var = jnp.mean(d * d, axis=-1, keepdims=True)
    return d * lax.rsqrt(var + LN_EPS) * g + b


def _resident(shape):
    nd = len(shape)
    return pl.BlockSpec(shape, lambda *_: (0,) * nd, pipeline_mode=pl.Buffered(1))


def _params(sem):
    return pltpu.CompilerParams(dimension_semantics=sem, vmem_limit_bytes=VMEM_LIMIT)


def _ffn_body(*refs, d_ff, chunk, alpha, ple):
    if ple:
        x_ref, wgu_ref, wd_ref, g_ref, b_ref, p_ref, pw_ref, gw_ref, gb_ref, o_ref = refs
    else:
        x_ref, wgu_ref, wd_ref, g_ref, b_ref, o_ref = refs
    x = x_ref[...]
    xb = x.astype(BF16)
    acc = jnp.zeros(x.shape, F32)
    for j in range(d_ff // chunk):
        a = _dot(xb, wgu_ref[:, j * chunk:(j + 1) * chunk])
        u = _dot(xb, wgu_ref[:, d_ff + j * chunk:d_ff + (j + 1) * chunk])
        h = (a * jax.nn.sigmoid(a) * u).astype(BF16)
        acc = acc + _dot(h, wd_ref[j * chunk:(j + 1) * chunk, :])
    y = _layer_norm(alpha * x + 0.5 * acc, g_ref[...], b_ref[...])
    if ple:
        gate = jax.nn.sigmoid(_dot(y.astype(BF16), gw_ref[...]) + gb_ref[...])
        y = y + gate * _dot(p_ref[...].astype(BF16), pw_ref[...])
    o_ref[...] = y


def _ffn_ln(x, wgu, wd, g, b, alpha, ple=None, tm=512, chunk=256):
    T, D = x.shape
    d_ff = wd.shape[0]
    row = lambda w: pl.BlockSpec((tm, w), lambda i: (i, 0))
    in_specs = [row(D), _resident(wgu.shape), _resident(wd.shape), _resident((1, D)), _resident((1, D))]
    args = [x, wgu, wd, g, b]
    if ple is not None:
        p, pw, gw, gb = ple
        in_specs += [row(p.shape[1]), _resident(pw.shape), _resident(gw.shape), _resident((1, D))]
        args += [p, pw, gw, gb]
    return pl.pallas_call(
        functools.partial(_ffn_body, d_ff=d_ff, chunk=chunk, alpha=alpha, ple=ple is not None),
        grid=(T // tm,),
        in_specs=in_specs,
        out_specs=row(D),
        out_shape=jax.ShapeDtypeStruct((T, D), F32),
        compiler_params=_params(("parallel",)),
        name="ffn_ln_ple" if ple is not None else "ffn_ln",
    )(*args)


A_NSA_Q, A_NSA_QR, A_SWA_QR = 0, 256, 512
A_KA, A_KB, A_VS, A_KW, A_VW = 768, 896, 1024, 1152, 1280
A_SK0, A_SK1, A_SV0, A_SV1 = 1408, 1536, 1664, 1792
A_WIDTH = 1920
P_HGRN = 1280
P_WIDTH = P_HGRN + 2048


def _rope(x, cosf, sinf, first_half):
    rot = jnp.where(first_half, pltpu.roll(x, 96, 1), pltpu.roll(x, 32, 1))
    return x * cosf + rot * sinf


def _proj_body(x_ref, w_ref, cos_ref, sin_ref, a_ref, h_ref, kc_ref, vc_ref, gate_ref, *, tm, seq, scale):
    xb = x_ref[...].astype(BF16)
    cosf = cos_ref[...]
    sinf = sin_ref[...]
    lane = lax.broadcasted_iota(jnp.int32, (tm, LANES), 1)
    first_half = (lane % HEAD_DIM) < (HEAD_DIM // 2)
    low = lane < HEAD_DIM
    pos = (pl.program_id(0) % (seq // tm)) * tm + lax.broadcasted_iota(jnp.int32, (tm, LANES), 0)
    blk = pos // SEL_BLOCK
    swap = lambda t: pltpu.roll(t, HEAD_DIM, 1)
    rope = lambda t: _rope(t, cosf, sinf, first_half)

    def put(c0, val):
        a_ref[:, c0:c0 + LANES] = val.astype(BF16)

    qq = _dot(xb, w_ref[:, 0:512])
    for half in range(2):
        nq = qq[:, half * LANES:(half + 1) * LANES]
        put(A_NSA_Q + half * LANES, nq * scale)
        put(A_NSA_QR + half * LANES, rope(nq) * scale)
        put(A_SWA_QR + half * LANES, rope(qq[:, 256 + half * LANES:256 + (half + 1) * LANES]) * scale)
    kv = _dot(xb, w_ref[:, 512:1024])
    sl, wn, sk, sv = (kv[:, i * LANES:(i + 1) * LANES] for i in range(4))
    sl_r = rope(sl)
    put(A_KA, jnp.where(low, sl_r, jnp.where(lane - HEAD_DIM == blk, 1.0, 0.0)))
    put(A_KB, jnp.where(low, jnp.where(lane == blk, 1.0, 0.0), swap(sl_r)))
    put(A_VS, jnp.where(low, swap(sl), 1.0))
    wn_r = rope(wn)
    put(A_KW, jnp.where(low, wn_r, swap(wn_r)))
    put(A_VW, jnp.where(low, swap(wn), 1.0))
    sk_r = rope(sk)
    sk_s = swap(sk_r)
    put(A_SK0, jnp.where(low, sk_r, sk_s))
    put(A_SK1, jnp.where(low, sk_s, sk_r))
    put(A_SV0, jnp.where(low, sv, 1.0))
    put(A_SV1, jnp.where(low, swap(sv), 1.0))
    cg = _dot(xb, w_ref[:, 1024:1280])
    cm = cg[:, :LANES]
    kc_ref[...] = cm[:, :HEAD_DIM]
    vc_ref[...] = swap(cm)[:, :HEAD_DIM]
    gate_ref[...] = jax.nn.sigmoid(cg[:, LANES:])
    for j in range(4):
        h_ref[:, j * 512:(j + 1) * 512] = _dot(xb, w_ref[:, P_HGRN + j * 512:P_HGRN + (j + 1) * 512])


def _proj_in(x, w, cosf, sinf, seq, tm=512):
    T, D = x.shape
    nseq = seq // tm
    row = lambda w_: pl.BlockSpec((tm, w_), lambda i: (i, 0))
    tab = pl.BlockSpec((tm, LANES), lambda i: (i % nseq, 0))
    return pl.pallas_call(
        functools.partial(_proj_body, tm=tm, seq=seq, scale=HEAD_DIM ** -0.5),
        grid=(T // tm,),
        in_specs=[row(D), _resident(w.shape), tab, tab],
        out_specs=[row(A_WIDTH), row(2048), row(HEAD_DIM), row(HEAD_DIM), row(LANES)],
        out_shape=[jax.ShapeDtypeStruct((T, A_WIDTH), BF16), jax.ShapeDtypeStruct((T, 2048), F32),
                   jax.ShapeDtypeStruct((T, HEAD_DIM), F32), jax.ShapeDtypeStruct((T, HEAD_DIM), F32),
                   jax.ShapeDtypeStruct((T, LANES), F32)],
        compiler_params=_params(("parallel",)),
        name="proj_in",
    )(x, w, cosf, sinf)


def _pack_w_in(w_in):
    offs = np.cumsum([0, 256, 64, 64, 64, 64, 64, 64, 12, 256, 128, 128, 512, 512, 512, 512])
    (nq, kcm, vcm, ksl, vsl, kwn, vwn, ngl, sq, sk, sv, hq, hf, hi, hg) = [
        w_in[:, offs[i]:offs[i + 1]] for i in range(15)]
    pad = jnp.zeros((w_in.shape[0], LANES - 12), w_in.dtype)
    w = jnp.concatenate([nq, sq, ksl, vsl, kwn, vwn, sk, sv, kcm, vcm, ngl, pad, hq, hf, hi, hg], axis=1).astype(BF16)
    assert w.shape[1] == P_WIDTH
    return w


def _cmp_body(ak_ref, av_ref, pos_ref, wk1_ref, wk2_ref, wv1_ref, wv2_ref, kc_ref, vc_ref):
    half = pos_ref.shape[1]
    for a_ref, w1_ref, w2_ref, o_ref in ((ak_ref, wk1_ref, wk2_ref, kc_ref), (av_ref, wv1_ref, wv2_ref, vc_ref)):
        a = a_ref[0]
        n = a.shape[0]
        p_lo = _dot((a + pos_ref[0:1, :]).astype(BF16), w1_ref[0:half, :])
        p_hi = _dot((a + pos_ref[1:2, :]).astype(BF16), w1_ref[half:2 * half, :])
        pre = p_lo + pltpu.roll(p_hi, n - 1, 0)
        act = (pre * jax.nn.sigmoid(pre)).astype(BF16)
        out = _dot(act, w2_ref[...])
        if o_ref is vc_ref:
            lane = lax.broadcasted_iota(jnp.int32, out.shape, 1)
            out = jnp.where(lane < HEAD_DIM, out, 1.0)
        o_ref[0] = out.astype(BF16)


def _compress(ak, av, pos2, wk1, wk2, wv1, wv2):
    B, n, w = ak.shape
    blk = pl.BlockSpec((1, n, w), lambda b: (b, 0, 0))
    out = pl.BlockSpec((1, n, LANES), lambda b: (b, 0, 0))
    return pl.pallas_call(
        _cmp_body,
        grid=(B,),
        in_specs=[blk, blk, _resident(pos2.shape), _resident(wk1.shape), _resident(wk2.shape),
                  _resident(wv1.shape), _resident(wv2.shape)],
        out_specs=[out, out],
        out_shape=[jax.ShapeDtypeStruct((B, n, LANES), BF16)] * 2,
        compiler_params=_params(("parallel",)),
        name="nsa_compress",
    )(ak, av, pos2, wk1, wk2, wv1, wv2)


def _stack_heads(q2, fill, low):
    c0, c1 = q2[:, :LANES], q2[:, LANES:]
    even = jnp.concatenate([jnp.where(low, c0, fill), jnp.where(low, c1, fill)], axis=0)
    odd = jnp.concatenate([jnp.where(low, fill, c0), jnp.where(low, fill, c1)], axis=0)
    return even, odd


def _band_mask(t, kstart, span, window):
    diff = t - (kstart + lax.broadcasted_iota(jnp.int32, (1, span), 1))
    return lax.bitcast_convert_type(diff, jnp.uint32) < jnp.uint32(window)


def _lane_tile_max(s):
    m = s[:, :LANES]
    for k in range(1, s.shape[1] // LANES):
        m = jnp.maximum(m, s[:, k * LANES:(k + 1) * LANES])
    return m


def _normalize(acc):
    return acc / pltpu.roll(acc, HEAD_DIM, 1)


def _nsa_body(q_ref, qr_ref, gate_ref, kc_ref, vc_ref, ovl_ref, ka_ref, kb_ref, vs_ref, kw_ref, vw_ref, o_ref,
              s_scr, m_scr, acc_scr, *, seq, top_n):
    TQ = ATTN_TQ
    n_sel = seq // SEL_BLOCK
    n_cmp = kc_ref.shape[1]
    start = pl.program_id(1) * TQ
    lane = lax.broadcasted_iota(jnp.int32, (TQ, LANES), 1)
    low = lane < HEAD_DIM
    zero = jnp.zeros((TQ, LANES), BF16)
    t1 = start + lax.broadcasted_iota(jnp.int32, (TQ, 1), 0)
    t4 = jnp.concatenate([t1, t1, t1, t1], axis=0)

    qe, qo = _stack_heads(q_ref[...], zero, low)
    qc = jnp.concatenate([qe, qo], axis=0)
    sc = _dot_nt(qc, kc_ref[0])
    cend = lax.broadcasted_iota(jnp.int32, (1, n_cmp), 1) * CMP_STRIDE + (CMP_LEN - 1)
    sc = jnp.where(cend <= t4, sc, NEG_INF)
    e = jnp.exp(sc - jnp.max(_lane_tile_max(sc), axis=-1, keepdims=True))
    pc = e / jnp.sum(e, axis=-1, keepdims=True)
    pc = jnp.where(t4 >= CMP_LEN - 1, pc, 0.0)
    o_cmp = _dot(pc.astype(BF16), vc_ref[0])
    pcs = pc[0:TQ] + pc[TQ:2 * TQ] + pc[2 * TQ:3 * TQ] + pc[3 * TQ:4 * TQ]
    ovl = ovl_ref[...]
    imp_t = sum(_dot_nt(ovl, piece) for piece in _split3(pcs))

    qr = qr_ref[...]
    span = NSA_WINDOW + TQ
    kstart = pl.multiple_of(jnp.maximum(start - NSA_WINDOW, 0), TQ)
    qwe, qwo = _stack_heads(qr, zero, low)
    sw = _dot_nt(jnp.concatenate([qwe, qwo], axis=0), kw_ref[pl.ds(kstart, span), :])
    sw = jnp.where(_band_mask(t4, kstart, span, NSA_WINDOW), sw, NEG_INF)
    pw = jnp.exp(sw - jnp.max(_lane_tile_max(sw), axis=-1, keepdims=True))
    o_win = _normalize(_dot(pw.astype(BF16), vw_ref[pl.ds(kstart, span), :]))

    j = lax.broadcasted_iota(jnp.int32, (n_sel, TQ), 0)
    blk_t = (start + lax.broadcasted_iota(jnp.int32, (n_sel, TQ), 1)) // SEL_BLOCK
    forced = (j == 0) | (j == blk_t) | (j == blk_t - 1)
    score = jnp.where(forced, SEL_FORCE, jnp.where(j <= blk_t, imp_t, -SEL_FORCE))
    groups = [score[8 * g:8 * g + 8, :] for g in range(n_sel // 8)]
    ranks = [jnp.zeros((8, TQ), F32) for _ in groups]
    sub = lax.broadcasted_iota(jnp.int32, (8, TQ), 0)
    for jp in range(n_sel):
        row = score[jp:jp + 1, :]
        for g, sg in enumerate(groups):
            if g > jp // 8:
                ranks[g] = ranks[g] + jnp.where(row >= sg, 1.0, 0.0)
            elif g < jp // 8:
                ranks[g] = ranks[g] + jnp.where(row > sg, 1.0, 0.0)
            else:
                tie = jnp.where(sub > jp % 8, 1.0, 0.0)
                ranks[g] = ranks[g] + jnp.where(row > sg, 1.0, 0.0) + jnp.where(row == sg, tie, 0.0)
    bias_t = jnp.where(jnp.concatenate(ranks, axis=0) < top_n, 0.0, -MASK_BIG)
    mdup = jnp.concatenate([bias_t] * (LANES // n_sel), axis=0).T.astype(BF16)

    qse, qso = _stack_heads(qr, mdup, low)
    n_chunks = (start + TQ + SEL_CK - 1) // SEL_CK
    m_scr[...] = jnp.full(m_scr.shape, NEG_INF, F32)
    acc_scr[...] = jnp.zeros(acc_scr.shape, F32)

    def scores(c, carry):
        ks = pl.multiple_of(c * SEL_CK, SEL_CK)
        s = jnp.concatenate([_dot_nt(qse, ka_ref[pl.ds(ks, SEL_CK), :]),
                             _dot_nt(qso, kb_ref[pl.ds(ks, SEL_CK), :])], axis=0)
        kpos = ks + lax.broadcasted_iota(jnp.int32, (1, SEL_CK), 1)
        s = jnp.where(kpos <= t4, s, NEG_INF)
        s_scr[c] = s
        m_scr[...] = jnp.maximum(m_scr[...], _lane_tile_max(s))
        return carry

    lax.fori_loop(0, n_chunks, scores, 0)
    m_sel = jnp.broadcast_to(jnp.max(m_scr[...], axis=-1, keepdims=True), (4 * TQ, LANES))

    def weighted(c, carry):
        ks = pl.multiple_of(c * SEL_CK, SEL_CK)
        p = jnp.concatenate([jnp.exp(s_scr[c, :, k * LANES:(k + 1) * LANES] - m_sel)
                             for k in range(SEL_CK // LANES)], axis=1)
        acc_scr[...] += _dot(p.astype(BF16), vs_ref[pl.ds(ks, SEL_CK), :])
        return carry

    lax.fori_loop(0, n_chunks, weighted, 0)
    o_slc = _normalize(acc_scr[...])

    g = gate_ref[...]
    heads = []
    for h, r in ((0, 0), (1, 2), (2, 1), (3, 3)):
        rows = slice(r * TQ, (r + 1) * TQ)
        heads.append(g[:, 3 * h:3 * h + 1] * o_cmp[rows] + g[:, 3 * h + 1:3 * h + 2] * o_slc[rows]
                     + g[:, 3 * h + 2:3 * h + 3] * o_win[rows])
    o_ref[...] = jnp.concatenate([jnp.where(low, heads[0], pltpu.roll(heads[1], HEAD_DIM, 1)),
                                  jnp.where(low, heads[2], pltpu.roll(heads[3], HEAD_DIM, 1))],
                                 axis=1).astype(o_ref.dtype)


def _nsa_attention(a, gates, kc, vc, ovl, batch, seq):
    T = a.shape[0]
    nq = seq // ATTN_TQ
    n_cmp = kc.shape[1]
    n_sel = seq // SEL_BLOCK
    assert LANES % n_sel == 0 and n_sel <= HEAD_DIM and seq >= NSA_WINDOW + ATTN_TQ and seq % SEL_CK == 0
    qspec = lambda cb: pl.BlockSpec((ATTN_TQ, 256), lambda b, i: (b * nq + i, cb))
    kspec = lambda c0: pl.BlockSpec((seq, LANES), lambda b, i: (b, c0 // LANES))
    cspec = pl.BlockSpec((1, n_cmp, LANES), lambda b, i: (b, 0, 0))
    return pl.pallas_call(
        functools.partial(_nsa_body, seq=seq, top_n=min(SEL_TOPN, n_sel)),
        grid=(batch, nq),
        in_specs=[qspec(A_NSA_Q // 256), qspec(A_NSA_QR // 256),
                  pl.BlockSpec((ATTN_TQ, LANES), lambda b, i: (b * nq + i, 0)),
                  cspec, cspec, pl.BlockSpec(ovl.shape, lambda b, i: (0, 0)),
                  kspec(A_KA), kspec(A_KB), kspec(A_VS), kspec(A_KW), kspec(A_VW)],
        out_specs=pl.BlockSpec((ATTN_TQ, 256), lambda b, i: (b * nq + i, 0)),
        out_shape=jax.ShapeDtypeStruct((T, 256), BF16),
        scratch_shapes=[pltpu.VMEM((seq // SEL_CK, 4 * ATTN_TQ, SEL_CK), F32),
                        pltpu.VMEM((4 * ATTN_TQ, LANES), F32), pltpu.VMEM((4 * ATTN_TQ, LANES), F32)],
        compiler_params=_params(("parallel", "arbitrary")),
        name="nsa_attention",
    )(a, a, gates, kc, vc, ovl, a, a, a, a, a)


def _overlap_t(seq):
    n_pad = seq // CMP_STRIDE
    n_cmp = (seq - CMP_LEN) // CMP_STRIDE + 1
    ci = np.arange(n_pad)[None, :] * CMP_STRIDE
    sj = np.arange(seq // SEL_BLOCK)[:, None] * SEL_BLOCK
    ovl = (ci < sj + SEL_BLOCK) & (ci + CMP_LEN > sj) & (np.arange(n_pad)[None, :] < n_cmp)
    return jnp.asarray(ovl.astype(np.float32), dtype=BF16)


SWA_STEP = 512


def _swa_body(sink_ref, q_ref, k0_ref, k1_ref, v0_ref, v1_ref, o_ref):
    TQ = ATTN_TQ
    span = SWA_WINDOW + TQ
    lane = lax.broadcasted_iota(jnp.int32, (TQ, LANES), 1)
    low = lane < HEAD_DIM
    zero = jnp.zeros((TQ, LANES), BF16)
    kv_refs = ((k0_ref, v0_ref), (k1_ref, v1_ref))
    sinks = [jnp.concatenate([jnp.full((TQ, LANES), sink_ref[2 * kv], F32),
                              jnp.full((TQ, LANES), sink_ref[2 * kv + 1], F32)], axis=0) for kv in range(SWA_KV)]
    kstarts, scores = [], []
    for sb in range(SWA_STEP // TQ):
        start = pl.program_id(1) * SWA_STEP + sb * TQ
        kstart = pl.multiple_of(jnp.maximum(start - SWA_WINDOW, 0), TQ)
        t1 = start + lax.broadcasted_iota(jnp.int32, (TQ, 1), 0)
        mask = _band_mask(jnp.concatenate([t1, t1], axis=0), kstart, span, SWA_WINDOW)
        q = q_ref[sb * TQ:(sb + 1) * TQ, :]
        for kv in range(SWA_KV):
            c = q[:, kv * LANES:(kv + 1) * LANES]
            qs = jnp.concatenate([jnp.where(low, c, zero), jnp.where(low, zero, c)], axis=0)
            scores.append(jnp.where(mask, _dot_nt(qs, kv_refs[kv][0][pl.ds(kstart, span), :]), NEG_INF))
            kstarts.append(kstart)
    maxes = [jnp.maximum(jnp.broadcast_to(jnp.max(_lane_tile_max(s), axis=-1, keepdims=True), (2 * TQ, LANES)),
                         sinks[n % SWA_KV]) for n, s in enumerate(scores)]
    accs = []
    for n, (s, m) in enumerate(zip(scores, maxes)):
        p = jnp.concatenate([jnp.exp(s[:, k * LANES:(k + 1) * LANES] - m) for k in range(span // LANES)], axis=1)
        accs.append(_dot(p.astype(BF16), kv_refs[n % SWA_KV][1][pl.ds(kstarts[n], span), :]))
    outs = [acc / (pltpu.roll(acc, HEAD_DIM, 1) + jnp.exp(sinks[n % SWA_KV] - m))
            for n, (acc, m) in enumerate(zip(accs, maxes))]
    for sb in range(SWA_STEP // TQ):
        cols = [jnp.where(low, o[:TQ], pltpu.roll(o[TQ:], HEAD_DIM, 1)) for o in outs[SWA_KV * sb:SWA_KV * (sb + 1)]]
        o_ref[sb * TQ:(sb + 1) * TQ, :] = jnp.concatenate(cols, axis=1).astype(o_ref.dtype)


def _swa_attention(a, sinks, batch, seq):
    T = a.shape[0]
    nq = seq // SWA_STEP
    assert seq >= SWA_WINDOW + ATTN_TQ and seq % SWA_STEP == 0
    kspec = lambda c0: pl.BlockSpec((seq, LANES), lambda b, i: (b, c0 // LANES))
    return pl.pallas_call(
        _swa_body,
        grid=(batch, nq),
        in_specs=[pl.BlockSpec(memory_space=pltpu.SMEM),
                  pl.BlockSpec((SWA_STEP, 256), lambda b, i: (b * nq + i, A_SWA_QR // 256)),
                  kspec(A_SK0), kspec(A_SK1), kspec(A_SV0), kspec(A_SV1)],
        out_specs=pl.BlockSpec((SWA_STEP, 256), lambda b, i: (b * nq + i, 0)),
        out_shape=jax.ShapeDtypeStruct((T, 256), BF16),
        compiler_params=_params(("parallel", "arbitrary")),
        name="swa_attention",
    )(sinks, a, a, a, a, a)


HGRN_LEVELS = (32, 16, 8, 4, 2, 1)


def _hgrn_tables():
    C = HGRN_CHUNK
    t = np.arange(C)[:, None]
    u = np.arange(C)[None, :]
    mats, masks = [], []
    for m in HGRN_LEVELS:
        r = (t // (2 * m)) * 2 * m + m - 1
        mats.append(np.where(t > r, (u > r) & (u <= t), (u > t) & (u <= r)))
        masks.append((t // (2 * m)) == (u // (2 * m)))
    mats.append(u <= t)
    mats.append(u > t)
    masks.append(t == u)
    amat = np.concatenate(mats, 0).astype(np.float32)
    return (jnp.asarray(np.concatenate([amat, amat, amat], 1), dtype=BF16),
            jnp.asarray(np.stack(masks).astype(np.float32)))


def _hgrn_body(q_ref, f_ref, i_ref, g_ref, lbraw_ref, ng_ref, amat_ref, pmask_ref, o_ref, state_ref,
               *, layer, rows):
    C, Dh = HGRN_CHUNK, HGRN_DIM
    nlev = len(HGRN_LEVELS)

    @pl.when(pl.program_id(1) == 0)
    def _():
        state_ref[...] = jnp.zeros(state_ref.shape, F32)

    raw = lbraw_ref[...]
    ex = jnp.exp(raw - jnp.max(raw, axis=0, keepdims=True))
    sm = ex / jnp.sum(ex, axis=0, keepdims=True)
    cs = sm[0:1]
    for k in range(1, layer + 1):
        cs = cs + sm[k:k + 1]
    lb = cs - sm[0:1]
    log_lb = jnp.log(jnp.maximum(lb, LOG_FLOOR))
    log_1m = jnp.log1p(-lb)
    row = lax.broadcasted_iota(jnp.int32, (C, HGRN_HEADS * Dh), 0)
    amat = amat_ref[...]

    heads = [slice(h * Dh, (h + 1) * Dh) for h in range(HGRN_HEADS)]

    def chunk(ci, carry):
        rws = pl.ds(pl.multiple_of(ci * C, C), C)
        z = f_ref[rws, :]
        q = q_ref[rws, :]
        v = i_ref[rws, :].astype(BF16)
        e = jnp.exp(-jnp.abs(z))
        inv = 1.0 / (1.0 + e)
        c = log_1m + (jnp.minimum(z, 0.0) - jnp.log(1.0 + e))
        lf = jnp.maximum(log_lb, c) + jnp.log(1.0 + jnp.exp(-jnp.abs(log_lb - c)))
        k = (1.0 - lb) * jnp.where(z > 0.0, e * inv, inv)
        dsum = _dot(amat, jnp.concatenate(_split3(lf), axis=0))
        qb, kb = q.astype(BF16), k.astype(BF16)
        atts = [pmask_ref[nlev] * _dot_nt(qb[:, s], kb[:, s]) for s in heads]
        for lev, m in enumerate(HGRN_LEVELS):
            dec = jnp.exp(dsum[lev * C:(lev + 1) * C])
            upper = (row // m) % 2 == 1
            qt = (jnp.where(upper, q, 0.0) * dec).astype(BF16)
            kt = (jnp.where(upper, 0.0, k) * dec).astype(BF16)
            pm = pmask_ref[lev]
            atts = [att + pm * _dot_nt(qt[:, s], kt[:, s]) for att, s in zip(atts, heads)]
        b = dsum[nlev * C:(nlev + 1) * C]
        rb = dsum[(nlev + 1) * C:(nlev + 2) * C]
        qhat = (q * jnp.exp(b)).astype(BF16)
        khat = (k * jnp.exp(rb)).astype(BF16)
        dlast = jnp.exp(b[C - 1:C, :])
        states = [state_ref[h] for h in range(HGRN_HEADS)]
        outs = [_dot(att.astype(BF16), v[:, s]) + _dot_nt(qhat[:, s], st.astype(BF16))
                for att, s, st in zip(atts, heads, states)]
        for h, (s, st) in enumerate(zip(heads, states)):
            state_ref[h] = st * dlast[:, s] + _dot_tn(v[:, s], khat[:, s])
        outs = [o * lax.rsqrt(jnp.mean(o * o, axis=-1, keepdims=True) + RMS_EPS) for o in outs]
        gg = g_ref[rws, :]
        o_ref[rws, :] = (jnp.concatenate(outs, axis=1) * ng_ref[...] * (gg * jax.nn.sigmoid(gg))).astype(o_ref.dtype)
        return carry

    lax.fori_loop(0, rows // C, chunk, 0)


def _hgrn(hf, lb_raw, norm_g, amat, pmask, layer, batch, seq, rows=512):
    T = hf.shape[0]
    W = HGRN_HEADS * HGRN_DIM
    ns = seq // rows
    spec = lambda cb: pl.BlockSpec((rows, W), lambda b, i: (b * ns + i, cb))
    return pl.pallas_call(
        functools.partial(_hgrn_body, layer=layer, rows=rows),
        grid=(batch, ns),
        in_specs=[spec(0), spec(1), spec(2), spec(3),
                  pl.BlockSpec(lb_raw.shape, lambda b, i: (0, 0)), pl.BlockSpec((1, W), lambda b, i: (0, 0)),
                  pl.BlockSpec(amat.shape, lambda b, i: (0, 0)), pl.BlockSpec(pmask.shape, lambda b, i: (0, 0, 0))],
        out_specs=pl.BlockSpec((rows, W), lambda b, i: (b * ns + i, 0)),
        out_shape=jax.ShapeDtypeStruct((T, W), BF16),
        scratch_shapes=[pltpu.VMEM((HGRN_HEADS, HGRN_DIM, HGRN_DIM), F32)],
        compiler_params=_params(("parallel", "arbitrary")),
        name="hgrn2",
    )(hf, hf, hf, hf, lb_raw, norm_g, amat, pmask)


def _out_body(x_ref, on_ref, os_ref, oh_ref, w_ref, g_ref, b_ref, o_ref, *, alpha):
    mix = jnp.concatenate([on_ref[...], os_ref[...], oh_ref[...]], axis=1)
    y = alpha * x_ref[...] + _dot(mix, w_ref[...])
    o_ref[...] = _layer_norm(y, g_ref[...], b_ref[...])


def _out_proj_ln(x, o_nsa, o_swa, o_hgrn, w, g, b, alpha, tm=512):
    T, D = x.shape
    row = lambda w_: pl.BlockSpec((tm, w_), lambda i: (i, 0))
    return pl.pallas_call(
        functools.partial(_out_body, alpha=alpha),
        grid=(T // tm,),
        in_specs=[row(D), row(o_nsa.shape[1]), row(o_swa.shape[1]), row(o_hgrn.shape[1]),
                  _resident(w.shape), _resident((1, D)), _resident((1, D))],
        out_specs=row(D),
        out_shape=jax.ShapeDtypeStruct((T, D), F32),
        compiler_params=_params(("parallel",)),
        name="out_proj_ln",
    )(x, o_nsa, o_swa, o_hgrn, w, g, b)


def _rope_tables(seq):
    inv = 1.0 / (ROPE_THETA ** (jnp.arange(0, HEAD_DIM, 2, dtype=F32) / HEAD_DIM))
    ang = jnp.arange(seq, dtype=F32)[:, None] * inv[None, :]
    cos, sin = jnp.cos(ang), jnp.sin(ang)
    return jnp.tile(cos, (1, 4)), jnp.concatenate([-sin, sin, -sin, sin], axis=1)


def kernel(x, p, ln_g, ln_b, ffn_w_gu, ffn_w_down, w_in, w_out, cmp_pos, cmp_k_w1, cmp_k_w2, cmp_v_w1, cmp_v_w2,
           swa_sinks, hgrn_lb_raw, hgrn_norm_g, ple_w, ple_gate_w, ple_gate_b):
    B, S, D = x.shape
    depth = w_in.shape[0]
    T = B * S
    alpha = (2.0 * depth) ** 0.25
    cosf, sinf = _rope_tables(S)
    ovl = _overlap_t(S)
    amat, pmask = _hgrn_tables()
    dup = lambda w: jnp.concatenate([w, w], axis=1).astype(BF16)
    xt = x.reshape(T, D)
    for i in range(depth):
        vec = lambda v: v.reshape(1, -1)
        xt = _ffn_ln(xt, ffn_w_gu[i, 0].astype(BF16), ffn_w_down[i, 0].astype(BF16),
                     vec(ln_g[i, 0]), vec(ln_b[i, 0]), alpha)
        a, hf, kcm, vcm, gates = _proj_in(xt, _pack_w_in(w_in[i]), cosf, sinf, S)
        n16 = S // CMP_STRIDE
        kc, vc = _compress(kcm.reshape(B, n16, CMP_STRIDE * HEAD_DIM), vcm.reshape(B, n16, CMP_STRIDE * HEAD_DIM),
                           cmp_pos[i].reshape(2, CMP_STRIDE * HEAD_DIM),
                           cmp_k_w1[i].astype(BF16), dup(cmp_k_w2[i]), cmp_v_w1[i].astype(BF16), dup(cmp_v_w2[i]))
        o_nsa = _nsa_attention(a, gates, kc, vc, ovl, B, S)
        o_swa = _swa_attention(a, swa_sinks[i], B, S)
        o_hgrn = _hgrn(hf, hgrn_lb_raw, vec(hgrn_norm_g[i]), amat, pmask, i, B, S)
        xt = _out_proj_ln(xt, o_nsa, o_swa, o_hgrn, w_out[i].astype(BF16), vec(ln_g[i, 1]), vec(ln_b[i, 1]), alpha)
        xt = _ffn_ln(xt, ffn_w_gu[i, 1].astype(BF16), ffn_w_down[i, 1].astype(BF16),
                     vec(ln_g[i, 2]), vec(ln_b[i, 2]), alpha,
                     ple=(p[i].reshape(T, -1), ple_w[i].astype(BF16), ple_gate_w[i].astype(BF16),
                          vec(ple_gate_b[i])))
    return xt.reshape(B, S, D)
```

```python
import functools

import numpy as np
import jax
import jax.numpy as jnp
from jax import lax
from jax.experimental import pallas as pl
from jax.experimental.pallas import tpu as pltpu

F32 = jnp.float32
BF16 = jnp.bfloat16

HEAD_DIM = 64
ROPE_THETA = 10000.0
LN_EPS = 1e-5
RMS_EPS = 1e-6
NEG_INF = -1e30
LOG_FLOOR = 1e-30
NSA_HEADS = 4
CMP_LEN = 32
CMP_STRIDE = 16
SEL_BLOCK = 64
SEL_TOPN = 16
SEL_FORCE = 1e4
NSA_WINDOW = 512
SWA_HEADS = 4
SWA_KV = 2
SWA_WINDOW = 128
HGRN_HEADS = 4
HGRN_DIM = 128
HGRN_CHUNK = 64

LANES = 128
VMEM_LIMIT = 56 * 1024 * 1024
MASK_BIG = 2.0 ** 100
ATTN_TQ = 128
NSA_TQ = 256
SEL_CK = 512

NT_DIMS = (((1,), (1,)), ((), ()))
TN_DIMS = (((0,), (0,)), ((), ()))


def _dot(a, b):
    return jnp.dot(a, b, preferred_element_type=F32)


def _dot_nt(a, b):
    return lax.dot_general(a, b, NT_DIMS, preferred_element_type=F32)


def _dot_tn(a, b):
    return lax.dot_general(a, b, TN_DIMS, preferred_element_type=F32)


def _split3(x):
    hi = x.astype(BF16)
    r1 = x - hi.astype(F32)
    mid = r1.astype(BF16)
    lo = (r1 - mid.astype(F32)).astype(BF16)
    return hi, mid, lo


def _layer_norm(y, g, b):
    mu = jnp.mean(y, axis=-1, keepdims=True)
    d = y - mu
    var = jnp.mean(d * d, axis=-1, keepdims=True)
    return d * lax.rsqrt(var + LN_EPS) * g + b


def _resident(shape):
    nd = len(shape)
    return pl.BlockSpec(shape, lambda *_: (0,) * nd, pipeline_mode=pl.Buffered(1))


def _params(sem):
    return pltpu.CompilerParams(dimension_semantics=sem, vmem_limit_bytes=VMEM_LIMIT)


def _ffn_body(*refs, d_ff, chunk, alpha, mixed):
    if mixed:
        (x_ref, on_ref, os_ref, oh_ref, wo_ref, g0_ref, b0_ref, wgu_ref, wd_ref, g_ref, b_ref,
         p_ref, pw_ref, gw_ref, gb_ref, o_ref) = refs
        mix = jnp.concatenate([on_ref[...], os_ref[...], oh_ref[...]], axis=1)
        x = _layer_norm(alpha * x_ref[...] + _dot(mix, wo_ref[...]), g0_ref[...], b0_ref[...])
    else:
        x_ref, wgu_ref, wd_ref, g_ref, b_ref, o_ref = refs
        x = x_ref[...]
    xb = x.astype(BF16)
    acc = jnp.zeros(x.shape, F32)
    for j in range(d_ff // chunk):
        a = _dot(xb, wgu_ref[:, j * chunk:(j + 1) * chunk])
        u = _dot(xb, wgu_ref[:, d_ff + j * chunk:d_ff + (j + 1) * chunk])
        h = (a * jax.nn.sigmoid(a) * u).astype(BF16)
        acc = acc + _dot(h, wd_ref[j * chunk:(j + 1) * chunk, :])
    y = _layer_norm(alpha * x + 0.5 * acc, g_ref[...], b_ref[...])
    if mixed:
        gate = jax.nn.sigmoid(_dot(y.astype(BF16), gw_ref[...]) + gb_ref[...])
        y = y + gate * _dot(p_ref[...].astype(BF16), pw_ref[...])
    o_ref[...] = y


def _ffn_ln(x, wgu, wd, g, b, alpha, mix=None, ple=None, tm=512, chunk=256):
    T, D = x.shape
    d_ff = wd.shape[0]
    row = lambda w: pl.BlockSpec((tm, w), lambda i: (i, 0))
    vec = _resident((1, D))
    in_specs = [row(D)]
    args = [x]
    if mix is not None:
        o_nsa, o_swa, o_hgrn, wo, g0, b0 = mix
        in_specs += [row(o_nsa.shape[1]), row(o_swa.shape[1]), row(o_hgrn.shape[1]), _resident(wo.shape), vec, vec]
        args += [o_nsa, o_swa, o_hgrn, wo, g0, b0]
    in_specs += [_resident(wgu.shape), _resident(wd.shape), vec, vec]
    args += [wgu, wd, g, b]
    if mix is not None:
        p, pw, gw, gb = ple
        in_specs += [row(p.shape[1]), _resident(pw.shape), _resident(gw.shape), vec]
        args += [p, pw, gw, gb]
    return pl.pallas_call(
        functools.partial(_ffn_body, d_ff=d_ff, chunk=chunk, alpha=alpha, mixed=mix is not None),
        grid=(T // tm,),
        in_specs=in_specs,
        out_specs=row(D),
        out_shape=jax.ShapeDtypeStruct((T, D), F32),
        compiler_params=_params(("parallel",)),
        name="mix_ffn_ln_ple" if mix is not None else "ffn_ln",
    )(*args)


A_NSA_Q, A_NSA_QR, A_SWA_QR = 0, 256, 512
A_KA, A_KB, A_VS, A_KW, A_VW = 768, 896, 1024, 1152, 1280
A_SK0, A_SK1, A_SV0, A_SV1 = 1408, 1536, 1664, 1792
A_WIDTH = 1920
P_HGRN = 1280
P_WIDTH = P_HGRN + 2048


def _rope(x, cosf, sinf, first_half):
    rot = jnp.where(first_half, pltpu.roll(x, 96, 1), pltpu.roll(x, 32, 1))
    return x * cosf + rot * sinf


def _proj_body(x_ref, w_ref, cos_ref, sin_ref, a_ref, h_ref, kc_ref, vc_ref, gate_ref, *, tm, seq, scale):
    xb = x_ref[...].astype(BF16)
    cosf = cos_ref[...]
    sinf = sin_ref[...]
    lane = lax.broadcasted_iota(jnp.int32, (tm, LANES), 1)
    first_half = (lane % HEAD_DIM) < (HEAD_DIM // 2)
    low = lane < HEAD_DIM
    pos = (pl.program_id(0) % (seq // tm)) * tm + lax.broadcasted_iota(jnp.int32, (tm, LANES), 0)
    blk = pos // SEL_BLOCK
    swap = lambda t: pltpu.roll(t, HEAD_DIM, 1)
    rope = lambda t: _rope(t, cosf, sinf, first_half)

    def put(c0, val):
        a_ref[:, c0:c0 + LANES] = val.astype(BF16)

    qq = _dot(xb, w_ref[:, 0:512])
    for half in range(2):
        nq = qq[:, half * LANES:(half + 1) * LANES]
        put(A_NSA_Q + half * LANES, nq * scale)
        put(A_NSA_QR + half * LANES, rope(nq) * scale)
        put(A_SWA_QR + half * LANES, rope(qq[:, 256 + half * LANES:256 + (half + 1) * LANES]) * scale)
    kv = _dot(xb, w_ref[:, 512:1024])
    sl, wn, sk, sv = (kv[:, i * LANES:(i + 1) * LANES] for i in range(4))
    sl_r = rope(sl)
    put(A_KA, jnp.where(low, sl_r, jnp.where(lane - HEAD_DIM == blk, 1.0, 0.0)))
    put(A_KB, jnp.where(low, jnp.where(lane == blk, 1.0, 0.0), swap(sl_r)))
    put(A_VS, jnp.where(low, swap(sl), 1.0))
    wn_r = rope(wn)
    put(A_KW, jnp.where(low, wn_r, swap(wn_r)))
    put(A_VW, jnp.where(low, swap(wn), 1.0))
    sk_r = rope(sk)
    sk_s = swap(sk_r)
    put(A_SK0, jnp.where(low, sk_r, sk_s))
    put(A_SK1, jnp.where(low, sk_s, sk_r))
    put(A_SV0, jnp.where(low, sv, 1.0))
    put(A_SV1, jnp.where(low, swap(sv), 1.0))
    cg = _dot(xb, w_ref[:, 1024:1280])
    cm = cg[:, :LANES]
    kc_ref[...] = cm[:, :HEAD_DIM]
    vc_ref[...] = swap(cm)[:, :HEAD_DIM]
    gate_ref[...] = jax.nn.sigmoid(cg[:, LANES:])
    for j in range(4):
        h_ref[:, j * 512:(j + 1) * 512] = _dot(xb, w_ref[:, P_HGRN + j * 512:P_HGRN + (j + 1) * 512])


def _proj_in(x, w, cosf, sinf, seq, tm=512):
    T, D = x.shape
    nseq = seq // tm
    row = lambda w_: pl.BlockSpec((tm, w_), lambda i: (i, 0))
    tab = pl.BlockSpec((tm, LANES), lambda i: (i % nseq, 0))
    return pl.pallas_call(
        functools.partial(_proj_body, tm=tm, seq=seq, scale=HEAD_DIM ** -0.5),
        grid=(T // tm,),
        in_specs=[row(D), _resident(w.shape), tab, tab],
        out_specs=[row(A_WIDTH), row(2048), row(HEAD_DIM), row(HEAD_DIM), row(LANES)],
        out_shape=[jax.ShapeDtypeStruct((T, A_WIDTH), BF16), jax.ShapeDtypeStruct((T, 2048), F32),
                   jax.ShapeDtypeStruct((T, HEAD_DIM), F32), jax.ShapeDtypeStruct((T, HEAD_DIM), F32),
                   jax.ShapeDtypeStruct((T, LANES), F32)],
        compiler_params=_params(("parallel",)),
        name="proj_in",
    )(x, w, cosf, sinf)


def _pack_w_in(w_in):
    offs = np.cumsum([0, 256, 64, 64, 64, 64, 64, 64, 12, 256, 128, 128, 512, 512, 512, 512])
    (nq, kcm, vcm, ksl, vsl, kwn, vwn, ngl, sq, sk, sv, hq, hf, hi, hg) = [
        w_in[:, offs[i]:offs[i + 1]] for i in range(15)]
    pad = jnp.zeros((w_in.shape[0], LANES - 12), w_in.dtype)
    w = jnp.concatenate([nq, sq, ksl, vsl, kwn, vwn, sk, sv, kcm, vcm, ngl, pad, hq, hf, hi, hg], axis=1).astype(BF16)
    assert w.shape[1] == P_WIDTH
    return w


def _cmp_body(ak_ref, av_ref, pos_ref, wk1_ref, wk2_ref, wv1_ref, wv2_ref, kc_ref, vc_ref):
    half = pos_ref.shape[1]
    for a_ref, w1_ref, w2_ref, o_ref in ((ak_ref, wk1_ref, wk2_ref, kc_ref), (av_ref, wv1_ref, wv2_ref, vc_ref)):
        a = a_ref[0]
        n = a.shape[0]
        p_lo = _dot((a + pos_ref[0:1, :]).astype(BF16), w1_ref[0:half, :])
        p_hi = _dot((a + pos_ref[1:2, :]).astype(BF16), w1_ref[half:2 * half, :])
        pre = p_lo + pltpu.roll(p_hi, n - 1, 0)
        act = (pre * jax.nn.sigmoid(pre)).astype(BF16)
        out = _dot(act, w2_ref[...])
        if o_ref is vc_ref:
            lane = lax.broadcasted_iota(jnp.int32, out.shape, 1)
            out = jnp.where(lane < HEAD_DIM, out, 1.0)
        o_ref[0] = out.astype(BF16)


def _compress(ak, av, pos2, wk1, wk2, wv1, wv2):
    B, n, w = ak.shape
    blk = pl.BlockSpec((1, n, w), lambda b: (b, 0, 0))
    out = pl.BlockSpec((1, n, LANES), lambda b: (b, 0, 0))
    return pl.pallas_call(
        _cmp_body,
        grid=(B,),
        in_specs=[blk, blk, _resident(pos2.shape), _resident(wk1.shape), _resident(wk2.shape),
                  _resident(wv1.shape), _resident(wv2.shape)],
        out_specs=[out, out],
        out_shape=[jax.ShapeDtypeStruct((B, n, LANES), BF16)] * 2,
        compiler_params=_params(("parallel",)),
        name="nsa_compress",
    )(ak, av, pos2, wk1, wk2, wv1, wv2)


def _stack_heads(q2, fill, low):
    c0, c1 = q2[:, :LANES], q2[:, LANES:]
    even = jnp.concatenate([jnp.where(low, c0, fill), jnp.where(low, c1, fill)], axis=0)
    odd = jnp.concatenate([jnp.where(low, fill, c0), jnp.where(low, fill, c1)], axis=0)
    return even, odd


def _band_mask(t, kstart, span, window):
    diff = t - (kstart + lax.broadcasted_iota(jnp.int32, (1, span), 1))
    return lax.bitcast_convert_type(diff, jnp.uint32) < jnp.uint32(window)


def _lane_tile_max(s):
    m = s[:, :LANES]
    for k in range(1, s.shape[1] // LANES):
        m = jnp.maximum(m, s[:, k * LANES:(k + 1) * LANES])
    return m


HEAD_BLOCK = ((0, 0), (1, 2), (2, 1), (3, 3))


def _normalize(acc):
    return acc / pltpu.roll(acc, HEAD_DIM, 1)


def _nsa_body(q_ref, qr_ref, gate_ref, kc_ref, vc_ref, ovl_ref, ka_ref, kb_ref, vs_ref, kw_ref, vw_ref, o_ref,
              s_scr, m_scr, acc_scr, *, seq, top_n):
    TQ = NSA_TQ
    n_sel = seq // SEL_BLOCK
    n_cmp = kc_ref.shape[1]
    start = pl.program_id(1) * TQ
    lane = lax.broadcasted_iota(jnp.int32, (TQ, LANES), 1)
    low = lane < HEAD_DIM
    zero = jnp.zeros((TQ, LANES), BF16)
    t1 = start + lax.broadcasted_iota(jnp.int32, (TQ, 1), 0)
    head_rows = [slice(r * TQ, (r + 1) * TQ) for r in range(NSA_HEADS)]
    row_max = lambda s: jnp.max(_lane_tile_max(s), axis=-1, keepdims=True)

    qe, qo = _stack_heads(q_ref[...], zero, low)
    sc = _dot_nt(jnp.concatenate([qe, qo], axis=0), kc_ref[0])
    qr = qr_ref[...]
    span = NSA_WINDOW + TQ
    kstart = pl.multiple_of(jnp.maximum(start - NSA_WINDOW, 0), TQ)
    qwe, qwo = _stack_heads(qr, zero, low)
    sw = _dot_nt(jnp.concatenate([qwe, qwo], axis=0), kw_ref[pl.ds(kstart, span), :])

    cend = lax.broadcasted_iota(jnp.int32, (1, n_cmp), 1) * CMP_STRIDE + (CMP_LEN - 1)
    cval = cend <= t1
    has_cmp = t1 >= CMP_LEN - 1
    scs = [jnp.where(cval, sc[r], NEG_INF) for r in head_rows]
    mcs = [row_max(s) for s in scs]
    ecs = [jnp.exp(s - m) for s, m in zip(scs, mcs)]
    pcs = [jnp.where(has_cmp, e / jnp.sum(e, axis=-1, keepdims=True), 0.0) for e in ecs]
    o_cmp = _dot(jnp.concatenate(pcs, axis=0).astype(BF16), vc_ref[0])
    ovl = ovl_ref[...]
    imp_t = sum(_dot_nt(ovl, piece) for piece in _split3(pcs[0] + pcs[1] + pcs[2] + pcs[3]))

    wmask = _band_mask(t1, kstart, span, NSA_WINDOW)
    window_out = []

    def window_pieces():
        sws = []
        for r in head_rows:
            sws.append(jnp.where(wmask, sw[r], NEG_INF))
            yield
        mws = []
        for s in sws:
            mws.append(jnp.broadcast_to(row_max(s), (TQ, LANES)))
            yield
        pws = []
        for s, m in zip(sws, mws):
            pws.append(jnp.concatenate([jnp.exp(s[:, k * LANES:(k + 1) * LANES] - m)
                                        for k in range(span // LANES)], axis=1).astype(BF16))
            yield
        window_out.append(_normalize(_dot(jnp.concatenate(pws, axis=0), vw_ref[pl.ds(kstart, span), :])))

    pieces = window_pieces()

    gates = gate_ref[...]
    gate_b = []

    def gate_broadcasts():
        for c in range(3 * NSA_HEADS):
            gate_b.append(jnp.broadcast_to(gates[:, c:c + 1], (TQ, LANES)))
            yield

    gate_pieces = gate_broadcasts()

    j = lax.broadcasted_iota(jnp.int32, (n_sel, TQ), 0)
    blk_t = (start + lax.broadcasted_iota(jnp.int32, (n_sel, TQ), 1)) // SEL_BLOCK
    forced = (j == 0) | (j == blk_t) | (j == blk_t - 1)
    score = jnp.where(forced, SEL_FORCE, jnp.where(j <= blk_t, imp_t, -SEL_FORCE))
    groups = [score[8 * g:8 * g + 8, :] for g in range(n_sel // 8)]
    ranks = [jnp.zeros((8, TQ), F32) for _ in groups]
    sub = lax.broadcasted_iota(jnp.int32, (8, TQ), 0)
    for jp in range(n_sel):
        row = score[jp:jp + 1, :]
        for g, sg in enumerate(groups):
            if g > jp // 8:
                ranks[g] = ranks[g] + jnp.where(row >= sg, 1.0, 0.0)
            elif g < jp // 8:
                ranks[g] = ranks[g] + jnp.where(row > sg, 1.0, 0.0)
            else:
                tie = jnp.where(sub > jp % 8, 1.0, 0.0)
                ranks[g] = ranks[g] + jnp.where(row > sg, 1.0, 0.0) + jnp.where(row == sg, tie, 0.0)
        if jp % 4 == 3:
            next(pieces, None)
        elif jp % 4 == 1:
            next(gate_pieces, None)
    for _ in pieces:
        pass
    for _ in gate_pieces:
        pass
    o_win = window_out[0]
    partial = [gate_b[3 * h] * o_cmp[head_rows[r]] + gate_b[3 * h + 2] * o_win[head_rows[r]] for h, r in HEAD_BLOCK]
    bias_t = jnp.where(jnp.concatenate(ranks, axis=0) < top_n, 0.0, -MASK_BIG)
    mdup = jnp.concatenate([bias_t] * (LANES // n_sel), axis=0).T.astype(BF16)

    qse, qso = _stack_heads(qr, mdup, low)
    n_pairs = (start + TQ + 2 * SEL_CK - 1) // (2 * SEL_CK)
    m_scr[...] = jnp.full(m_scr.shape, NEG_INF, F32)
    acc_scr[...] = jnp.zeros(acc_scr.shape, F32)

    def scores(cp, carry):
        raw, tile_max = [], []
        for u in range(2):
            ks = pl.multiple_of((2 * cp + u) * SEL_CK, SEL_CK)
            raw.append((ks, _dot_nt(qse, ka_ref[pl.ds(ks, SEL_CK), :]), _dot_nt(qso, kb_ref[pl.ds(ks, SEL_CK), :])))
        for u, (ks, se, so) in enumerate(raw):
            causal = ks + lax.broadcasted_iota(jnp.int32, (1, SEL_CK), 1) <= t1
            masked = [jnp.where(causal, blk, NEG_INF) for blk in (se[:TQ], se[TQ:], so[:TQ], so[TQ:])]
            s_scr[2 * cp + u] = jnp.concatenate(masked, axis=0)
            tile_max.append(jnp.concatenate([_lane_tile_max(blk) for blk in masked], axis=0))
        pair_max = jnp.max(jnp.maximum(tile_max[0], tile_max[1]), axis=-1, keepdims=True)
        m_scr[...] = jnp.maximum(m_scr[...], jnp.broadcast_to(pair_max, m_scr.shape))
        return carry

    lax.fori_loop(0, n_pairs, scores, 0)
    m_sel = m_scr[...]

    def weighted(cp, carry):
        ks = pl.multiple_of(cp * 2 * SEL_CK, 2 * SEL_CK)
        p = jnp.concatenate([jnp.exp(s_scr[2 * cp + u, :, k * LANES:(k + 1) * LANES] - m_sel)
                             for u in range(2) for k in range(SEL_CK // LANES)], axis=1)
        acc_scr[...] += _dot(p.astype(BF16), vs_ref[pl.ds(ks, 2 * SEL_CK), :])
        return carry

    lax.fori_loop(0, n_pairs, weighted, 0)
    o_slc = _normalize(acc_scr[...])

    heads = [partial[h] + gate_b[3 * h + 1] * o_slc[head_rows[r]] for h, r in HEAD_BLOCK]
    o_ref[...] = jnp.concatenate([jnp.where(low, heads[0], pltpu.roll(heads[1], HEAD_DIM, 1)),
                                  jnp.where(low, heads[2], pltpu.roll(heads[3], HEAD_DIM, 1))],
                                 axis=1).astype(o_ref.dtype)


def _nsa_attention(a, gates, kc, vc, ovl, batch, seq):
    T = a.shape[0]
    nq = seq // NSA_TQ
    n_cmp = kc.shape[1]
    n_sel = seq // SEL_BLOCK
    assert LANES % n_sel == 0 and n_sel <= HEAD_DIM and seq >= NSA_WINDOW + NSA_TQ and seq % (2 * SEL_CK) == 0
    qspec = lambda cb: pl.BlockSpec((NSA_TQ, 256), lambda b, i: (b * nq + i, cb))
    kspec = lambda c0: pl.BlockSpec((seq, LANES), lambda b, i: (b, c0 // LANES))
    cspec = pl.BlockSpec((1, n_cmp, LANES), lambda b, i: (b, 0, 0))
    return pl.pallas_call(
        functools.partial(_nsa_body, seq=seq, top_n=min(SEL_TOPN, n_sel)),
        grid=(batch, nq),
        in_specs=[qspec(A_NSA_Q // 256), qspec(A_NSA_QR // 256),
                  pl.BlockSpec((NSA_TQ, LANES), lambda b, i: (b * nq + i, 0)),
                  cspec, cspec, pl.BlockSpec(ovl.shape, lambda b, i: (0, 0)),
                  kspec(A_KA), kspec(A_KB), kspec(A_VS), kspec(A_KW), kspec(A_VW)],
        out_specs=pl.BlockSpec((NSA_TQ, 256), lambda b, i: (b * nq + i, 0)),
        out_shape=jax.ShapeDtypeStruct((T, 256), BF16),
        scratch_shapes=[pltpu.VMEM((seq // SEL_CK, 4 * NSA_TQ, SEL_CK), F32),
                        pltpu.VMEM((4 * NSA_TQ, LANES), F32), pltpu.VMEM((4 * NSA_TQ, LANES), F32)],
        compiler_params=_params(("parallel", "arbitrary")),
        name="nsa_attention",
    )(a, a, gates, kc, vc, ovl, a, a, a, a, a)


def _overlap_t(seq):
    n_pad = seq // CMP_STRIDE
    n_cmp = (seq - CMP_LEN) // CMP_STRIDE + 1
    ci = np.arange(n_pad)[None, :] * CMP_STRIDE
    sj = np.arange(seq // SEL_BLOCK)[:, None] * SEL_BLOCK
    ovl = (ci < sj + SEL_BLOCK) & (ci + CMP_LEN > sj) & (np.arange(n_pad)[None, :] < n_cmp)
    return jnp.asarray(ovl.astype(np.float32), dtype=BF16)


SWA_STEP = 512


def _swa_body(sink_ref, q_ref, k0_ref, k1_ref, v0_ref, v1_ref, o_ref):
    TQ = ATTN_TQ
    span = SWA_WINDOW + TQ
    lane = lax.broadcasted_iota(jnp.int32, (TQ, LANES), 1)
    low = lane < HEAD_DIM
    zero = jnp.zeros((TQ, LANES), BF16)
    kv_refs = ((k0_ref, v0_ref), (k1_ref, v1_ref))
    sinks = [jnp.concatenate([jnp.full((TQ, LANES), sink_ref[2 * kv], F32),
                              jnp.full((TQ, LANES), sink_ref[2 * kv + 1], F32)], axis=0) for kv in range(SWA_KV)]
    kstarts, scores = [], []
    for sb in range(SWA_STEP // TQ):
        start = pl.program_id(1) * SWA_STEP + sb * TQ
        kstart = pl.multiple_of(jnp.maximum(start - SWA_WINDOW, 0), TQ)
        t1 = start + lax.broadcasted_iota(jnp.int32, (TQ, 1), 0)
        mask = _band_mask(jnp.concatenate([t1, t1], axis=0), kstart, span, SWA_WINDOW)
        q = q_ref[sb * TQ:(sb + 1) * TQ, :]
        for kv in range(SWA_KV):
            c = q[:, kv * LANES:(kv + 1) * LANES]
            qs = jnp.concatenate([jnp.where(low, c, zero), jnp.where(low, zero, c)], axis=0)
            scores.append(jnp.where(mask, _dot_nt(qs, kv_refs[kv][0][pl.ds(kstart, span), :]), NEG_INF))
            kstarts.append(kstart)
    maxes = [jnp.maximum(jnp.broadcast_to(jnp.max(_lane_tile_max(s), axis=-1, keepdims=True), (2 * TQ, LANES)),
                         sinks[n % SWA_KV]) for n, s in enumerate(scores)]
    accs = []
    for n, (s, m) in enumerate(zip(scores, maxes)):
        p = jnp.concatenate([jnp.exp(s[:, k * LANES:(k + 1) * LANES] - m) for k in range(span // LANES)], axis=1)
        accs.append(_dot(p.astype(BF16), kv_refs[n % SWA_KV][1][pl.ds(kstarts[n], span), :]))
    outs = [acc / (pltpu.roll(acc, HEAD_DIM, 1) + jnp.exp(sinks[n % SWA_KV] - m))
            for n, (acc, m) in enumerate(zip(accs, maxes))]
    for sb in range(SWA_STEP // TQ):
        cols = [jnp.where(low, o[:TQ], pltpu.roll(o[TQ:], HEAD_DIM, 1)) for o in outs[SWA_KV * sb:SWA_KV * (sb + 1)]]
        o_ref[sb * TQ:(sb + 1) * TQ, :] = jnp.concatenate(cols, axis=1).astype(o_ref.dtype)


def _swa_attention(a, sinks, batch, seq):
    T = a.shape[0]
    nq = seq // SWA_STEP
    assert seq >= SWA_WINDOW + ATTN_TQ and seq % SWA_STEP == 0
    kspec = lambda c0: pl.BlockSpec((seq, LANES), lambda b, i: (b, c0 // LANES))
    return pl.pallas_call(
        _swa_body,
        grid=(batch, nq),
        in_specs=[pl.BlockSpec(memory_space=pltpu.SMEM),
                  pl.BlockSpec((SWA_STEP, 256), lambda b, i: (b * nq + i, A_SWA_QR // 256)),
                  kspec(A_SK0), kspec(A_SK1), kspec(A_SV0), kspec(A_SV1)],
        out_specs=pl.BlockSpec((SWA_STEP, 256), lambda b, i: (b * nq + i, 0)),
        out_shape=jax.ShapeDtypeStruct((T, 256), BF16),
        compiler_params=_params(("parallel", "arbitrary")),
        name="swa_attention",
    )(sinks, a, a, a, a, a)


HGRN_LEVELS = (32, 16, 8, 4, 2, 1)
HGRN_SUB = 2


def _hgrn_tables():
    C = HGRN_CHUNK
    t = np.arange(C)[:, None]
    u = np.arange(C)[None, :]
    mats, masks = [], []
    for m in HGRN_LEVELS:
        r = (t // (2 * m)) * 2 * m + m - 1
        mats.append(np.where(t > r, (u > r) & (u <= t), (u > t) & (u <= r)))
        masks.append((t // (2 * m)) == (u // (2 * m)))
    mats.append(u <= t)
    mats.append(u > t)
    masks.append(t == u)
    amat = np.concatenate(mats, 0).astype(np.float32)
    return (jnp.asarray(np.concatenate([amat, amat, amat], 1), dtype=BF16),
            jnp.asarray(np.stack(masks).astype(np.float32)))


def _hgrn_body(q_ref, f_ref, i_ref, g_ref, lbraw_ref, ng_ref, amat_ref, pmask_ref, o_ref, state_ref,
               *, layer, rows):
    C, Dh = HGRN_CHUNK, HGRN_DIM
    nlev = len(HGRN_LEVELS)

    @pl.when(pl.program_id(1) == 0)
    def _():
        state_ref[...] = jnp.zeros(state_ref.shape, F32)

    raw = lbraw_ref[...]
    ex = jnp.exp(raw - jnp.max(raw, axis=0, keepdims=True))
    sm = ex / jnp.sum(ex, axis=0, keepdims=True)
    cs = sm[0:1]
    for k in range(1, layer + 1):
        cs = cs + sm[k:k + 1]
    lb = cs - sm[0:1]
    log_lb = jnp.log(jnp.maximum(lb, LOG_FLOOR))
    log_1m = jnp.log1p(-lb)
    W = HGRN_HEADS * Dh
    row = lax.broadcasted_iota(jnp.int32, (C, HGRN_SUB * W), 0)
    amat = amat_ref[...]
    wide = lambda t: jnp.concatenate([t] * HGRN_SUB, axis=1)
    lb_w, log_lb_w, log_1m_w = wide(lb), wide(log_lb), wide(log_1m)
    blocks = [slice(n * Dh, (n + 1) * Dh) for n in range(HGRN_SUB * HGRN_HEADS)]

    def step(ci, carry):
        rws = [pl.ds(pl.multiple_of((ci * HGRN_SUB + u) * C, C), C) for u in range(HGRN_SUB)]
        side = lambda ref: jnp.concatenate([ref[r, :] for r in rws], axis=1)
        z, q, gg = side(f_ref), side(q_ref), side(g_ref)
        v = side(i_ref).astype(BF16)
        e = jnp.exp(-jnp.abs(z))
        inv = 1.0 / (1.0 + e)
        c = log_1m_w + (jnp.minimum(z, 0.0) - jnp.log(1.0 + e))
        lf = jnp.maximum(log_lb_w, c) + jnp.log(1.0 + jnp.exp(-jnp.abs(log_lb_w - c)))
        k = (1.0 - lb_w) * jnp.where(z > 0.0, e * inv, inv)
        dsum = _dot(amat, jnp.concatenate(_split3(lf), axis=0))
        qb, kb = q.astype(BF16), k.astype(BF16)
        atts = [pmask_ref[nlev] * _dot_nt(qb[:, s], kb[:, s]) for s in blocks]
        for lev, m in enumerate(HGRN_LEVELS):
            dec = jnp.exp(dsum[lev * C:(lev + 1) * C])
            upper = (row // m) % 2 == 1
            qt = (jnp.where(upper, q, 0.0) * dec).astype(BF16)
            kt = (jnp.where(upper, 0.0, k) * dec).astype(BF16)
            pm = pmask_ref[lev]
            atts = [att + pm * _dot_nt(qt[:, s], kt[:, s]) for att, s in zip(atts, blocks)]
        b = dsum[nlev * C:(nlev + 1) * C]
        rb = dsum[(nlev + 1) * C:(nlev + 2) * C]
        qhat = (q * jnp.exp(b)).astype(BF16)
        khat = (k * jnp.exp(rb)).astype(BF16)
        dlast = jnp.exp(b[C - 1:C, :])
        intra = [_dot(att.astype(BF16), v[:, s]) for att, s in zip(atts, blocks)]
        update = [_dot_tn(v[:, s], khat[:, s]) for s in blocks]
        states = [state_ref[h] for h in range(HGRN_HEADS)]
        outs = []
        for u in range(HGRN_SUB):
            blk = blocks[u * HGRN_HEADS:(u + 1) * HGRN_HEADS]
            outs += [intra[u * HGRN_HEADS + h] + _dot_nt(qhat[:, s], states[h].astype(BF16)) for h, s in enumerate(blk)]
            states = [st * dlast[:, s] + update[u * HGRN_HEADS + h] for h, (s, st) in enumerate(zip(blk, states))]
        for h, st in enumerate(states):
            state_ref[h] = st
        outs = [o * lax.rsqrt(jnp.mean(o * o, axis=-1, keepdims=True) + RMS_EPS) for o in outs]
        gated = jnp.concatenate(outs, axis=1) * wide(ng_ref[...]) * (gg * jax.nn.sigmoid(gg))
        for u, r in enumerate(rws):
            o_ref[r, :] = gated[:, u * W:(u + 1) * W].astype(o_ref.dtype)
        return carry

    lax.fori_loop(0, rows // (C * HGRN_SUB), step, 0)


def _hgrn(hf, lb_raw, norm_g, amat, pmask, layer, batch, seq, rows=512):
    T = hf.shape[0]
    W = HGRN_HEADS * HGRN_DIM
    ns = seq // rows
    spec = lambda cb: pl.BlockSpec((rows, W), lambda b, i: (b * ns + i, cb))
    return pl.pallas_call(
        functools.partial(_hgrn_body, layer=layer, rows=rows),
        grid=(batch, ns),
        in_specs=[spec(0), spec(1), spec(2), spec(3),
                  pl.BlockSpec(lb_raw.shape, lambda b, i: (0, 0)), pl.BlockSpec((1, W), lambda b, i: (0, 0)),
                  pl.BlockSpec(amat.shape, lambda b, i: (0, 0)), pl.BlockSpec(pmask.shape, lambda b, i: (0, 0, 0))],
        out_specs=pl.BlockSpec((rows, W), lambda b, i: (b * ns + i, 0)),
        out_shape=jax.ShapeDtypeStruct((T, W), BF16),
        scratch_shapes=[pltpu.VMEM((HGRN_HEADS, HGRN_DIM, HGRN_DIM), F32)],
        compiler_params=_params(("parallel", "arbitrary")),
        name="hgrn2",
    )(hf, hf, hf, hf, lb_raw, norm_g, amat, pmask)


def _rope_tables(seq):
    inv = 1.0 / (ROPE_THETA ** (jnp.arange(0, HEAD_DIM, 2, dtype=F32) / HEAD_DIM))
    ang = jnp.arange(seq, dtype=F32)[:, None] * inv[None, :]
    cos, sin = jnp.cos(ang), jnp.sin(ang)
    return jnp.tile(cos, (1, 4)), jnp.concatenate([-sin, sin, -sin, sin], axis=1)


def kernel(x, p, ln_g, ln_b, ffn_w_gu, ffn_w_down, w_in, w_out, cmp_pos, cmp_k_w1, cmp_k_w2, cmp_v_w1, cmp_v_w2,
           swa_sinks, hgrn_lb_raw, hgrn_norm_g, ple_w, ple_gate_w, ple_gate_b):
    B, S, D = x.shape
    depth = w_in.shape[0]
    T = B * S
    alpha = (2.0 * depth) ** 0.25
    cosf, sinf = _rope_tables(S)
    ovl = _overlap_t(S)
    amat, pmask = _hgrn_tables()
    dup = lambda w: jnp.concatenate([w, w], axis=1).astype(BF16)
    xt = x.reshape(T, D)
    for i in range(depth):
        vec = lambda v: v.reshape(1, -1)
        xt = _ffn_ln(xt, ffn_w_gu[i, 0].astype(BF16), ffn_w_down[i, 0].astype(BF16),
                     vec(ln_g[i, 0]), vec(ln_b[i, 0]), alpha)
        a, hf, kcm, vcm, gates = _proj_in(xt, _pack_w_in(w_in[i]), cosf, sinf, S)
        n16 = S // CMP_STRIDE
        kc, vc = _compress(kcm.reshape(B, n16, CMP_STRIDE * HEAD_DIM), vcm.reshape(B, n16, CMP_STRIDE * HEAD_DIM),
                           cmp_pos[i].reshape(2, CMP_STRIDE * HEAD_DIM),
                           cmp_k_w1[i].astype(BF16), dup(cmp_k_w2[i]), cmp_v_w1[i].astype(BF16), dup(cmp_v_w2[i]))
        o_nsa = _nsa_attention(a, gates, kc, vc, ovl, B, S)
        o_swa = _swa_attention(a, swa_sinks[i], B, S)
        o_hgrn = _hgrn(hf, hgrn_lb_raw, vec(hgrn_norm_g[i]), amat, pmask, i, B, S)
        xt = _ffn_ln(xt, ffn_w_gu[i, 1].astype(BF16), ffn_w_down[i, 1].astype(BF16),
                     vec(ln_g[i, 2]), vec(ln_b[i, 2]), alpha,
                     mix=(o_nsa, o_swa, o_hgrn, w_out[i].astype(BF16), vec(ln_g[i, 1]), vec(ln_b[i, 1])),
                     ple=(p[i].reshape(T, -1), ple_w[i].astype(BF16), ple_gate_w[i].astype(BF16),
                          vec(ple_gate_b[i])))
    return xt.reshape(B, S, D)
```

```python
import functools

import numpy as np
import jax
import jax.numpy as jnp
from jax import lax
from jax.experimental import pallas as pl
from jax.experimental.pallas import tpu as pltpu

F32 = jnp.float32
BF16 = jnp.bfloat16

HEAD_DIM = 64
ROPE_THETA = 10000.0
LN_EPS = 1e-5
RMS_EPS = 1e-6
NEG_INF = -1e30
LOG_FLOOR = 1e-30
NSA_HEADS = 4
CMP_LEN = 32
CMP_STRIDE = 16
SEL_BLOCK = 64
SEL_TOPN = 16
SEL_FORCE = 1e4
NSA_WINDOW = 512
SWA_HEADS = 4
SWA_KV = 2
SWA_WINDOW = 128
HGRN_HEADS = 4
HGRN_DIM = 128
HGRN_CHUNK = 64

LANES = 128
VMEM_LIMIT = 56 * 1024 * 1024
MASK_BIG = 2.0 ** 100
ATTN_TQ = 128
NSA_TQ = 256
SEL_CK = 512

NT_DIMS = (((1,), (1,)), ((), ()))
TN_DIMS = (((0,), (0,)), ((), ()))


def _dot(a, b):
    return jnp.dot(a, b, preferred_element_type=F32)


def _dot_nt(a, b):
    return lax.dot_general(a, b, NT_DIMS, preferred_element_type=F32)


def _dot_tn(a, b):
    return lax.dot_general(a, b, TN_DIMS, preferred_element_type=F32)


def _split3(x):
    hi = x.astype(BF16)
    r1 = x - hi.astype(F32)
    mid = r1.astype(BF16)
    lo = (r1 - mid.astype(F32)).astype(BF16)
    return hi, mid, lo


def _layer_norm(y, g, b):
    mu = jnp.mean(y, axis=-1, keepdims=True)
    d = y - mu
    var = jnp.mean(d * d, axis=-1, keepdims=True)
    return d * lax.rsqrt(var + LN_EPS) * g + b


def _pick(arr, *lead):
    return arr, lead


def _resident(op):
    arr, lead = op
    tail = arr.shape[len(lead):]
    return pl.BlockSpec((None,) * len(lead) + tail, lambda *_: lead + (0,) * len(tail), pipeline_mode=pl.Buffered(1))


def _rows(op, tm):
    arr, lead = op
    return pl.BlockSpec((None,) * len(lead) + (tm, arr.shape[-1]), lambda i, *_: lead + (i, 0))


def _params(sem):
    return pltpu.CompilerParams(dimension_semantics=sem, vmem_limit_bytes=VMEM_LIMIT)


def _ffn_body(*refs, d_ff, chunk, alpha, mixed):
    if mixed:
        (x_ref, on_ref, os_ref, oh_ref, wo_ref, g0_ref, b0_ref, wgu_ref, wd_ref, g_ref, b_ref,
         p_ref, pw_ref, gw_ref, gb_ref, o_ref) = refs
        mix = jnp.concatenate([on_ref[...], os_ref[...], oh_ref[...]], axis=1)
        x = _layer_norm(alpha * x_ref[...] + _dot(mix, wo_ref[...]), g0_ref[...], b0_ref[...])
    else:
        x_ref, wgu_ref, wd_ref, g_ref, b_ref, o_ref = refs
        x = x_ref[...]
    xb = x.astype(BF16)
    acc = jnp.zeros(x.shape, F32)
    for j in range(d_ff // chunk):
        a = _dot(xb, wgu_ref[:, j * chunk:(j + 1) * chunk])
        u = _dot(xb, wgu_ref[:, d_ff + j * chunk:d_ff + (j + 1) * chunk])
        h = (a * jax.nn.sigmoid(a) * u).astype(BF16)
        acc = acc + _dot(h, wd_ref[j * chunk:(j + 1) * chunk, :])
    y = _layer_norm(alpha * x + 0.5 * acc, g_ref[...], b_ref[...])
    if mixed:
        gate = jax.nn.sigmoid(_dot(y.astype(BF16), gw_ref[...]) + gb_ref[...])
        y = y + gate * _dot(p_ref[...].astype(BF16), pw_ref[...])
    o_ref[...] = y


def _ffn_ln(x, wgu, wd, g, b, alpha, mix=None, ple=None, tm=512, chunk=256):
    T, D = x.shape
    d_ff = wd[0].shape[-2]
    ops = [(_pick(x), _rows)]
    if mix is not None:
        o_nsa, o_swa, o_hgrn, wo, g0, b0 = mix
        ops += [(_pick(o_nsa), _rows), (_pick(o_swa), _rows), (_pick(o_hgrn), _rows),
                (wo, _resident), (g0, _resident), (b0, _resident)]
    ops += [(wgu, _resident), (wd, _resident), (g, _resident), (b, _resident)]
    if mix is not None:
        p, pw, gw, gb = ple
        ops += [(p, _rows), (pw, _resident), (gw, _resident), (gb, _resident)]
    return pl.pallas_call(
        functools.partial(_ffn_body, d_ff=d_ff, chunk=chunk, alpha=alpha, mixed=mix is not None),
        grid=(T // tm,),
        in_specs=[spec(op, tm) if spec is _rows else spec(op) for op, spec in ops],
        out_specs=pl.BlockSpec((tm, D), lambda i: (i, 0)),
        out_shape=jax.ShapeDtypeStruct((T, D), F32),
        compiler_params=_params(("parallel",)),
        name="mix_ffn_ln_ple" if mix is not None else "ffn_ln",
    )(*[op[0] for op, _ in ops])


A_NSA_Q, A_NSA_QR, A_SWA_QR = 0, 256, 512
A_KA, A_KB, A_VS, A_KW, A_VW = 768, 896, 1024, 1152, 1280
A_SK0, A_SK1, A_SV0, A_SV1 = 1408, 1536, 1664, 1792
A_WIDTH = 1920
P_HGRN = 1280
P_WIDTH = P_HGRN + 2048


def _rope(x, cosf, sinf, first_half):
    rot = jnp.where(first_half, pltpu.roll(x, 96, 1), pltpu.roll(x, 32, 1))
    return x * cosf + rot * sinf


def _proj_body(x_ref, w_ref, cos_ref, sin_ref, a_ref, h_ref, kc_ref, vc_ref, gate_ref, *, tm, seq, scale):
    xb = x_ref[...].astype(BF16)
    cosf = cos_ref[...]
    sinf = sin_ref[...]
    lane = lax.broadcasted_iota(jnp.int32, (tm, LANES), 1)
    first_half = (lane % HEAD_DIM) < (HEAD_DIM // 2)
    low = lane < HEAD_DIM
    pos = (pl.program_id(0) % (seq // tm)) * tm + lax.broadcasted_iota(jnp.int32, (tm, LANES), 0)
    blk = pos // SEL_BLOCK
    swap = lambda t: pltpu.roll(t, HEAD_DIM, 1)
    rope = lambda t: _rope(t, cosf, sinf, first_half)

    def put(c0, val):
        a_ref[:, c0:c0 + LANES] = val.astype(BF16)

    mm = lambda c0, c1: _dot_nt(xb, w_ref[c0:c1, :])
    qq = mm(0, 512)
    for half in range(2):
        nq = qq[:, half * LANES:(half + 1) * LANES]
        put(A_NSA_Q + half * LANES, nq * scale)
        put(A_NSA_QR + half * LANES, rope(nq) * scale)
        put(A_SWA_QR + half * LANES, rope(qq[:, 256 + half * LANES:256 + (half + 1) * LANES]) * scale)
    kv = mm(512, 1024)
    sl, wn, sk, sv = (kv[:, i * LANES:(i + 1) * LANES] for i in range(4))
    sl_r = rope(sl)
    put(A_KA, jnp.where(low, sl_r, jnp.where(lane - HEAD_DIM == blk, 1.0, 0.0)))
    put(A_KB, jnp.where(low, jnp.where(lane == blk, 1.0, 0.0), swap(sl_r)))
    put(A_VS, jnp.where(low, swap(sl), 1.0))
    wn_r = rope(wn)
    put(A_KW, jnp.where(low, wn_r, swap(wn_r)))
    put(A_VW, jnp.where(low, swap(wn), 1.0))
    sk_r = rope(sk)
    sk_s = swap(sk_r)
    put(A_SK0, jnp.where(low, sk_r, sk_s))
    put(A_SK1, jnp.where(low, sk_s, sk_r))
    put(A_SV0, jnp.where(low, sv, 1.0))
    put(A_SV1, jnp.where(low, swap(sv), 1.0))
    cg = mm(1024, 1280)
    cm = cg[:, :LANES]
    kc_ref[...] = cm[:, :HEAD_DIM]
    vc_ref[...] = swap(cm)[:, :HEAD_DIM]
    gate_ref[...] = jax.nn.sigmoid(cg[:, LANES:])
    for j in range(4):
        h_ref[:, j * 512:(j + 1) * 512] = mm(P_HGRN + j * 512, P_HGRN + (j + 1) * 512)


def _proj_in(x, w, cosf, sinf, seq, tm=512):
    T, D = x.shape
    nseq = seq // tm
    row = lambda w_: pl.BlockSpec((tm, w_), lambda i: (i, 0))
    tab = pl.BlockSpec((tm, LANES), lambda i: (i % nseq, 0))
    return pl.pallas_call(
        functools.partial(_proj_body, tm=tm, seq=seq, scale=HEAD_DIM ** -0.5),
        grid=(T // tm,),
        in_specs=[row(D), _resident(w), tab, tab],
        out_specs=[row(A_WIDTH), row(2048), row(HEAD_DIM), row(HEAD_DIM), row(LANES)],
        out_shape=[jax.ShapeDtypeStruct((T, A_WIDTH), BF16), jax.ShapeDtypeStruct((T, 2048), F32),
                   jax.ShapeDtypeStruct((T, HEAD_DIM), F32), jax.ShapeDtypeStruct((T, HEAD_DIM), F32),
                   jax.ShapeDtypeStruct((T, LANES), F32)],
        compiler_params=_params(("parallel",)),
        name="proj_in",
    )(x, w[0], cosf, sinf)


def _pack_w_in(w_in):
    wt = jnp.swapaxes(w_in, 1, 2)
    offs = np.cumsum([0, 256, 64, 64, 64, 64, 64, 64, 12, 256, 128, 128, 512, 512, 512, 512])
    (nq, kcm, vcm, ksl, vsl, kwn, vwn, ngl, sq, sk, sv, hq, hf, hi, hg) = [
        wt[:, offs[i]:offs[i + 1], :] for i in range(15)]
    pad = jnp.zeros((wt.shape[0], LANES - 12, wt.shape[2]), wt.dtype)
    w = jnp.concatenate([nq, sq, ksl, vsl, kwn, vwn, sk, sv, kcm, vcm, ngl, pad, hq, hf, hi, hg], axis=1).astype(BF16)
    assert w.shape[1] == P_WIDTH
    return w


def _cmp_body(ak_ref, av_ref, pos_ref, wk1_ref, wk2_ref, wv1_ref, wv2_ref, kc_ref, vc_ref):
    half = pos_ref.shape[1]
    for a_ref, w1_ref, w2_ref, o_ref in ((ak_ref, wk1_ref, wk2_ref, kc_ref), (av_ref, wv1_ref, wv2_ref, vc_ref)):
        a = a_ref[0]
        n = a.shape[0]
        p_lo = _dot((a + pos_ref[0:1, :]).astype(BF16), w1_ref[0:half, :])
        p_hi = _dot((a + pos_ref[1:2, :]).astype(BF16), w1_ref[half:2 * half, :])
        pre = p_lo + pltpu.roll(p_hi, n - 1, 0)
        act = (pre * jax.nn.sigmoid(pre)).astype(BF16)
        out = _dot(act, w2_ref[...])
        if o_ref is vc_ref:
            lane = lax.broadcasted_iota(jnp.int32, out.shape, 1)
            out = jnp.where(lane < HEAD_DIM, out, 1.0)
        o_ref[0] = out.astype(BF16)


def _compress(ak, av, pos2, wk1, wk2, wv1, wv2):
    B, n, w = ak.shape
    blk = pl.BlockSpec((1, n, w), lambda b: (b, 0, 0))
    out = pl.BlockSpec((1, n, LANES), lambda b: (b, 0, 0))
    return pl.pallas_call(
        _cmp_body,
        grid=(B,),
        in_specs=[blk, blk] + [_resident(op) for op in (pos2, wk1, wk2, wv1, wv2)],
        out_specs=[out, out],
        out_shape=[jax.ShapeDtypeStruct((B, n, LANES), BF16)] * 2,
        compiler_params=_params(("parallel",)),
        name="nsa_compress",
    )(ak, av, pos2[0], wk1[0], wk2[0], wv1[0], wv2[0])


def _stack_heads(q2, fill, low):
    c0, c1 = q2[:, :LANES], q2[:, LANES:]
    even = jnp.concatenate([jnp.where(low, c0, fill), jnp.where(low, c1, fill)], axis=0)
    odd = jnp.concatenate([jnp.where(low, fill, c0), jnp.where(low, fill, c1)], axis=0)
    return even, odd


def _band_mask(t, kstart, span, window):
    diff = t - (kstart + lax.broadcasted_iota(jnp.int32, (1, span), 1))
    return lax.bitcast_convert_type(diff, jnp.uint32) < jnp.uint32(window)


def _lane_tile_max(s):
    m = s[:, :LANES]
    for k in range(1, s.shape[1] // LANES):
        m = jnp.maximum(m, s[:, k * LANES:(k + 1) * LANES])
    return m


HEAD_BLOCK = ((0, 0), (1, 2), (2, 1), (3, 3))


def _normalize(acc):
    return acc / pltpu.roll(acc, HEAD_DIM, 1)


def _nsa_body(q_ref, qr_ref, gate_ref, kc_ref, vc_ref, ovl_ref, ka_ref, kb_ref, vs_ref, kw_ref, vw_ref, o_ref,
              s_scr, m_scr, acc_scr, *, seq, top_n):
    TQ = NSA_TQ
    n_sel = seq // SEL_BLOCK
    n_cmp = kc_ref.shape[1]
    start = pl.program_id(1) * TQ
    lane = lax.broadcasted_iota(jnp.int32, (TQ, LANES), 1)
    low = lane < HEAD_DIM
    zero = jnp.zeros((TQ, LANES), BF16)
    t1 = start + lax.broadcasted_iota(jnp.int32, (TQ, 1), 0)
    head_rows = [slice(r * TQ, (r + 1) * TQ) for r in range(NSA_HEADS)]
    row_max = lambda s: jnp.max(_lane_tile_max(s), axis=-1, keepdims=True)

    qe, qo = _stack_heads(q_ref[...], zero, low)
    sc = _dot_nt(jnp.concatenate([qe, qo], axis=0), kc_ref[0])
    qr = qr_ref[...]
    span = NSA_WINDOW + TQ
    kstart = pl.multiple_of(jnp.maximum(start - NSA_WINDOW, 0), TQ)
    qwe, qwo = _stack_heads(qr, zero, low)
    sw = _dot_nt(jnp.concatenate([qwe, qwo], axis=0), kw_ref[pl.ds(kstart, span), :])

    cend = lax.broadcasted_iota(jnp.int32, (1, n_cmp), 1) * CMP_STRIDE + (CMP_LEN - 1)
    cval = cend <= t1
    has_cmp = t1 >= CMP_LEN - 1
    scs = [jnp.where(cval, sc[r], NEG_INF) for r in head_rows]
    mcs = [row_max(s) for s in scs]
    ecs = [jnp.exp(s - m) for s, m in zip(scs, mcs)]
    pcs = [jnp.where(has_cmp, e / jnp.sum(e, axis=-1, keepdims=True), 0.0) for e in ecs]
    o_cmp = _dot(jnp.concatenate(pcs, axis=0).astype(BF16), vc_ref[0])
    ovl = ovl_ref[...]
    imp_t = sum(_dot_nt(ovl, piece) for piece in _split3(pcs[0] + pcs[1] + pcs[2] + pcs[3]))

    wmask = _band_mask(t1, kstart, span, NSA_WINDOW)
    window_out = []

    def window_pieces():
        sws = []
        for r in head_rows:
            sws.append(jnp.where(wmask, sw[r], NEG_INF))
            yield
        mws = []
        for s in sws:
            mws.append(jnp.broadcast_to(row_max(s), (TQ, LANES)))
            yield
        pws = []
        for s, m in zip(sws, mws):
            pws.append(jnp.concatenate([jnp.exp(s[:, k * LANES:(k + 1) * LANES] - m)
                                        for k in range(span // LANES)], axis=1).astype(BF16))
            yield
        window_out.append(_normalize(_dot(jnp.concatenate(pws, axis=0), vw_ref[pl.ds(kstart, span), :])))

    pieces = window_pieces()

    gates = gate_ref[...]
    gate_b = []

    def gate_broadcasts():
        for c in range(3 * NSA_HEADS):
            gate_b.append(jnp.broadcast_to(gates[:, c:c + 1], (TQ, LANES)))
            yield

    gate_pieces = gate_broadcasts()

    j = lax.broadcasted_iota(jnp.int32, (n_sel, TQ), 0)
    blk_t = (start + lax.broadcasted_iota(jnp.int32, (n_sel, TQ), 1)) // SEL_BLOCK
    forced = (j == 0) | (j == blk_t) | (j == blk_t - 1)
    score = jnp.where(forced, SEL_FORCE, jnp.where(j <= blk_t, imp_t, -SEL_FORCE))
    groups = [score[8 * g:8 * g + 8, :] for g in range(n_sel // 8)]
    ranks = [jnp.zeros((8, TQ), F32) for _ in groups]
    sub = lax.broadcasted_iota(jnp.int32, (8, TQ), 0)
    for jp in range(n_sel):
        row = score[jp:jp + 1, :]
        for g, sg in enumerate(groups):
            if g > jp // 8:
                ranks[g] = ranks[g] + jnp.where(row >= sg, 1.0, 0.0)
            elif g < jp // 8:
                ranks[g] = ranks[g] + jnp.where(row > sg, 1.0, 0.0)
            else:
                tie = jnp.where(sub > jp % 8, 1.0, 0.0)
                ranks[g] = ranks[g] + jnp.where(row > sg, 1.0, 0.0) + jnp.where(row == sg, tie, 0.0)
        if jp % 4 == 3:
            next(pieces, None)
        elif jp % 4 == 1:
            next(gate_pieces, None)
    for _ in pieces:
        pass
    for _ in gate_pieces:
        pass
    o_win = window_out[0]
    partial = [gate_b[3 * h] * o_cmp[head_rows[r]] + gate_b[3 * h + 2] * o_win[head_rows[r]] for h, r in HEAD_BLOCK]
    bias_t = jnp.where(jnp.concatenate(ranks, axis=0) < top_n, 0.0, -MASK_BIG)
    mdup = jnp.concatenate([bias_t] * (LANES // n_sel), axis=0).T.astype(BF16)

    qse, qso = _stack_heads(qr, mdup, low)
    n_pairs = (start + TQ + 2 * SEL_CK - 1) // (2 * SEL_CK)
    m_scr[...] = jnp.full(m_scr.shape, NEG_INF, F32)
    acc_scr[...] = jnp.zeros(acc_scr.shape, F32)

    def scores(cp, carry):
        raw, tile_max = [], []
        for u in range(2):
            ks = pl.multiple_of((2 * cp + u) * SEL_CK, SEL_CK)
            raw.append((ks, _dot_nt(qse, ka_ref[pl.ds(ks, SEL_CK), :]), _dot_nt(qso, kb_ref[pl.ds(ks, SEL_CK), :])))
        for u, (ks, se, so) in enumerate(raw):
            causal = ks + lax.broadcasted_iota(jnp.int32, (1, SEL_CK), 1) <= t1
            masked = [jnp.where(causal, blk, NEG_INF) for blk in (se[:TQ], se[TQ:], so[:TQ], so[TQ:])]
            s_scr[2 * cp + u] = jnp.concatenate(masked, axis=0)
            tile_max.append(jnp.concatenate([_lane_tile_max(blk) for blk in masked], axis=0))
        pair_max = jnp.max(jnp.maximum(tile_max[0], tile_max[1]), axis=-1, keepdims=True)
        m_scr[...] = jnp.maximum(m_scr[...], jnp.broadcast_to(pair_max, m_scr.shape))
        return carry

    lax.fori_loop(0, n_pairs, scores, 0)
    m_sel = m_scr[...]

    def weighted(cp, carry):
        ks = pl.multiple_of(cp * 2 * SEL_CK, 2 * SEL_CK)
        p = jnp.concatenate([jnp.exp(s_scr[2 * cp + u, :, k * LANES:(k + 1) * LANES] - m_sel)
                             for u in range(2) for k in range(SEL_CK // LANES)], axis=1)
        acc_scr[...] += _dot(p.astype(BF16), vs_ref[pl.ds(ks, 2 * SEL_CK), :])
        return carry

    lax.fori_loop(0, n_pairs, weighted, 0)
    o_slc = _normalize(acc_scr[...])

    heads = [partial[h] + gate_b[3 * h + 1] * o_slc[head_rows[r]] for h, r in HEAD_BLOCK]
    o_ref[...] = jnp.concatenate([jnp.where(low, heads[0], pltpu.roll(heads[1], HEAD_DIM, 1)),
                                  jnp.where(low, heads[2], pltpu.roll(heads[3], HEAD_DIM, 1))],
                                 axis=1).astype(o_ref.dtype)


def _nsa_attention(a, gates, kc, vc, ovl, batch, seq):
    T = a.shape[0]
    nq = seq // NSA_TQ
    n_cmp = kc.shape[1]
    n_sel = seq // SEL_BLOCK
    assert LANES % n_sel == 0 and n_sel <= HEAD_DIM and seq >= NSA_WINDOW + NSA_TQ and seq % (2 * SEL_CK) == 0
    qspec = lambda cb: pl.BlockSpec((NSA_TQ, 256), lambda b, i: (b * nq + i, cb))
    kspec = lambda c0: pl.BlockSpec((seq, LANES), lambda b, i: (b, c0 // LANES))
    cspec = pl.BlockSpec((1, n_cmp, LANES), lambda b, i: (b, 0, 0))
    return pl.pallas_call(
        functools.partial(_nsa_body, seq=seq, top_n=min(SEL_TOPN, n_sel)),
        grid=(batch, nq),
        in_specs=[qspec(A_NSA_Q // 256), qspec(A_NSA_QR // 256),
                  pl.BlockSpec((NSA_TQ, LANES), lambda b, i: (b * nq + i, 0)),
                  cspec, cspec, pl.BlockSpec(ovl.shape, lambda b, i: (0, 0)),
                  kspec(A_KA), kspec(A_KB), kspec(A_VS), kspec(A_KW), kspec(A_VW)],
        out_specs=pl.BlockSpec((NSA_TQ, 256), lambda b, i: (b * nq + i, 0)),
        out_shape=jax.ShapeDtypeStruct((T, 256), BF16),
        scratch_shapes=[pltpu.VMEM((seq // SEL_CK, 4 * NSA_TQ, SEL_CK), F32),
                        pltpu.VMEM((4 * NSA_TQ, LANES), F32), pltpu.VMEM((4 * NSA_TQ, LANES), F32)],
        compiler_params=_params(("parallel", "arbitrary")),
        name="nsa_attention",
    )(a, a, gates, kc, vc, ovl, a, a, a, a, a)


def _overlap_t(seq):
    n_pad = seq // CMP_STRIDE
    n_cmp = (seq - CMP_LEN) // CMP_STRIDE + 1
    ci = np.arange(n_pad)[None, :] * CMP_STRIDE
    sj = np.arange(seq // SEL_BLOCK)[:, None] * SEL_BLOCK
    ovl = (ci < sj + SEL_BLOCK) & (ci + CMP_LEN > sj) & (np.arange(n_pad)[None, :] < n_cmp)
    return jnp.asarray(ovl.astype(np.float32), dtype=BF16)


SWA_STEP = 512


def _swa_body(sink_ref, q_ref, k0_ref, k1_ref, v0_ref, v1_ref, o_ref, *, layer):
    TQ = ATTN_TQ
    span = SWA_WINDOW + TQ
    lane = lax.broadcasted_iota(jnp.int32, (TQ, LANES), 1)
    low = lane < HEAD_DIM
    zero = jnp.zeros((TQ, LANES), BF16)
    kv_refs = ((k0_ref, v0_ref), (k1_ref, v1_ref))
    sinks = [jnp.concatenate([jnp.full((TQ, LANES), sink_ref[layer, 2 * kv], F32),
                              jnp.full((TQ, LANES), sink_ref[layer, 2 * kv + 1], F32)], axis=0)
             for kv in range(SWA_KV)]
    kstarts, scores = [], []
    for sb in range(SWA_STEP // TQ):
        start = pl.program_id(1) * SWA_STEP + sb * TQ
        kstart = pl.multiple_of(jnp.maximum(start - SWA_WINDOW, 0), TQ)
        t1 = start + lax.broadcasted_iota(jnp.int32, (TQ, 1), 0)
        mask = _band_mask(jnp.concatenate([t1, t1], axis=0), kstart, span, SWA_WINDOW)
        q = q_ref[sb * TQ:(sb + 1) * TQ, :]
        for kv in range(SWA_KV):
            c = q[:, kv * LANES:(kv + 1) * LANES]
            qs = jnp.concatenate([jnp.where(low, c, zero), jnp.where(low, zero, c)], axis=0)
            scores.append(jnp.where(mask, _dot_nt(qs, kv_refs[kv][0][pl.ds(kstart, span), :]), NEG_INF))
            kstarts.append(kstart)
    maxes = [jnp.maximum(jnp.broadcast_to(jnp.max(_lane_tile_max(s), axis=-1, keepdims=True), (2 * TQ, LANES)),
                         sinks[n % SWA_KV]) for n, s in enumerate(scores)]
    accs = []
    for n, (s, m) in enumerate(zip(scores, maxes)):
        p = jnp.concatenate([jnp.exp(s[:, k * LANES:(k + 1) * LANES] - m) for k in range(span // LANES)], axis=1)
        accs.append(_dot(p.astype(BF16), kv_refs[n % SWA_KV][1][pl.ds(kstarts[n], span), :]))
    outs = [acc / (pltpu.roll(acc, HEAD_DIM, 1) + jnp.exp(sinks[n % SWA_KV] - m))
            for n, (acc, m) in enumerate(zip(accs, maxes))]
    for sb in range(SWA_STEP // TQ):
        cols = [jnp.where(low, o[:TQ], pltpu.roll(o[TQ:], HEAD_DIM, 1)) for o in outs[SWA_KV * sb:SWA_KV * (sb + 1)]]
        o_ref[sb * TQ:(sb + 1) * TQ, :] = jnp.concatenate(cols, axis=1).astype(o_ref.dtype)


def _swa_attention(a, sinks, layer, batch, seq):
    T = a.shape[0]
    nq = seq // SWA_STEP
    assert seq >= SWA_WINDOW + ATTN_TQ and seq % SWA_STEP == 0
    kspec = lambda c0: pl.BlockSpec((seq, LANES), lambda b, i: (b, c0 // LANES))
    return pl.pallas_call(
        functools.partial(_swa_body, layer=layer),
        grid=(batch, nq),
        in_specs=[pl.BlockSpec(memory_space=pltpu.SMEM),
                  pl.BlockSpec((SWA_STEP, 256), lambda b, i: (b * nq + i, A_SWA_QR // 256)),
                  kspec(A_SK0), kspec(A_SK1), kspec(A_SV0), kspec(A_SV1)],
        out_specs=pl.BlockSpec((SWA_STEP, 256), lambda b, i: (b * nq + i, 0)),
        out_shape=jax.ShapeDtypeStruct((T, 256), BF16),
        compiler_params=_params(("parallel", "arbitrary")),
        name="swa_attention",
    )(sinks, a, a, a, a, a)


HGRN_LEVELS = (32, 16, 8, 4, 2, 1)
HGRN_SUB = 2


def _hgrn_tables():
    C = HGRN_CHUNK
    t = np.arange(C)[:, None]
    u = np.arange(C)[None, :]
    mats, masks = [], []
    for m in HGRN_LEVELS:
        r = (t // (2 * m)) * 2 * m + m - 1
        mats.append(np.where(t > r, (u > r) & (u <= t), (u > t) & (u <= r)))
        masks.append((t // (2 * m)) == (u // (2 * m)))
    mats.append(u <= t)
    mats.append(u > t)
    masks.append(t == u)
    amat = np.concatenate(mats, 0).astype(np.float32)
    return (jnp.asarray(np.concatenate([amat, amat, amat], 1), dtype=BF16),
            jnp.asarray(np.stack(masks).astype(np.float32)))


def _hgrn_body(q_ref, f_ref, i_ref, g_ref, lbraw_ref, ng_ref, amat_ref, pmask_ref, o_ref, state_ref,
               *, layer, rows):
    C, Dh = HGRN_CHUNK, HGRN_DIM
    nlev = len(HGRN_LEVELS)

    @pl.when(pl.program_id(1) == 0)
    def _():
        state_ref[...] = jnp.zeros(state_ref.shape, F32)

    raw = lbraw_ref[...]
    ex = jnp.exp(raw - jnp.max(raw, axis=0, keepdims=True))
    sm = ex / jnp.sum(ex, axis=0, keepdims=True)
    cs = sm[0:1]
    for k in range(1, layer + 1):
        cs = cs + sm[k:k + 1]
    lb = cs - sm[0:1]
    log_lb = jnp.log(jnp.maximum(lb, LOG_FLOOR))
    log_1m = jnp.log1p(-lb)
    W = HGRN_HEADS * Dh
    row = lax.broadcasted_iota(jnp.int32, (C, HGRN_SUB * W), 0)
    amat = amat_ref[...]
    wide = lambda t: jnp.concatenate([t] * HGRN_SUB, axis=1)
    lb_w, log_lb_w, log_1m_w = wide(lb), wide(log_lb), wide(log_1m)
    blocks = [slice(n * Dh, (n + 1) * Dh) for n in range(HGRN_SUB * HGRN_HEADS)]

    def step(ci, carry):
        rws = [pl.ds(pl.multiple_of((ci * HGRN_SUB + u) * C, C), C) for u in range(HGRN_SUB)]
        side = lambda ref: jnp.concatenate([ref[r, :] for r in rws], axis=1)
        z, q, gg = side(f_ref), side(q_ref), side(g_ref)
        v = side(i_ref).astype(BF16)
        e = jnp.exp(-jnp.abs(z))
        inv = 1.0 / (1.0 + e)
        c = log_1m_w + (jnp.minimum(z, 0.0) - jnp.log(1.0 + e))
        lf = jnp.maximum(log_lb_w, c) + jnp.log(1.0 + jnp.exp(-jnp.abs(log_lb_w - c)))
        k = (1.0 - lb_w) * jnp.where(z > 0.0, e * inv, inv)
        dsum = _dot(amat, jnp.concatenate(_split3(lf), axis=0))
        qb, kb = q.astype(BF16), k.astype(BF16)
        atts = [pmask_ref[nlev] * _dot_nt(qb[:, s], kb[:, s]) for s in blocks]
        for lev, m in enumerate(HGRN_LEVELS):
            dec = jnp.exp(dsum[lev * C:(lev + 1) * C])
            upper = (row // m) % 2 == 1
            qt = (jnp.where(upper, q, 0.0) * dec).astype(BF16)
            kt = (jnp.where(upper, 0.0, k) * dec).astype(BF16)
            pm = pmask_ref[lev]
            atts = [att + pm * _dot_nt(qt[:, s], kt[:, s]) for att, s in zip(atts, blocks)]
        b = dsum[nlev * C:(nlev + 1) * C]
        rb = dsum[(nlev + 1) * C:(nlev + 2) * C]
        qhat = (q * jnp.exp(b)).astype(BF16)
        khat = (k * jnp.exp(rb)).astype(BF16)
        dlast = jnp.exp(b[C - 1:C, :])
        intra = [_dot(att.astype(BF16), v[:, s]) for att, s in zip(atts, blocks)]
        update = [_dot_tn(v[:, s], khat[:, s]) for s in blocks]
        states = [state_ref[h] for h in range(HGRN_HEADS)]
        outs = []
        for u in range(HGRN_SUB):
            blk = blocks[u * HGRN_HEADS:(u + 1) * HGRN_HEADS]
            outs += [intra[u * HGRN_HEADS + h] + _dot_nt(qhat[:, s], states[h].astype(BF16)) for h, s in enumerate(blk)]
            states = [st * dlast[:, s] + update[u * HGRN_HEADS + h] for h, (s, st) in enumerate(zip(blk, states))]
        for h, st in enumerate(states):
            state_ref[h] = st
        outs = [o * lax.rsqrt(jnp.mean(o * o, axis=-1, keepdims=True) + RMS_EPS) for o in outs]
        gated = jnp.concatenate(outs, axis=1) * wide(ng_ref[...]) * (gg * jax.nn.sigmoid(gg))
        for u, r in enumerate(rws):
            o_ref[r, :] = gated[:, u * W:(u + 1) * W].astype(o_ref.dtype)
        return carry

    lax.fori_loop(0, rows // (C * HGRN_SUB), step, 0)


def _hgrn(hf, lb_raw, norm_g, amat, pmask, layer, batch, seq, rows=512):
    T = hf.shape[0]
    W = HGRN_HEADS * HGRN_DIM
    ns = seq // rows
    spec = lambda cb: pl.BlockSpec((rows, W), lambda b, i: (b * ns + i, cb))
    return pl.pallas_call(
        functools.partial(_hgrn_body, layer=layer, rows=rows),
        grid=(batch, ns),
        in_specs=[spec(0), spec(1), spec(2), spec(3),
                  pl.BlockSpec(lb_raw.shape, lambda b, i: (0, 0)), pl.BlockSpec((None, 1, W), lambda b, i: (layer, 0, 0)),
                  pl.BlockSpec(amat.shape, lambda b, i: (0, 0)), pl.BlockSpec(pmask.shape, lambda b, i: (0, 0, 0))],
        out_specs=pl.BlockSpec((rows, W), lambda b, i: (b * ns + i, 0)),
        out_shape=jax.ShapeDtypeStruct((T, W), BF16),
        scratch_shapes=[pltpu.VMEM((HGRN_HEADS, HGRN_DIM, HGRN_DIM), F32)],
        compiler_params=_params(("parallel", "arbitrary")),
        name="hgrn2",
    )(hf, hf, hf, hf, lb_raw, norm_g, amat, pmask)


def _rope_tables(seq):
    inv = 1.0 / (ROPE_THETA ** (jnp.arange(0, HEAD_DIM, 2, dtype=F32) / HEAD_DIM))
    ang = jnp.arange(seq, dtype=F32)[:, None] * inv[None, :]
    cos, sin = jnp.cos(ang), jnp.sin(ang)
    return jnp.tile(cos, (1, 4)), jnp.concatenate([-sin, sin, -sin, sin], axis=1)


def kernel(x, p, ln_g, ln_b, ffn_w_gu, ffn_w_down, w_in, w_out, cmp_pos, cmp_k_w1, cmp_k_w2, cmp_v_w1, cmp_v_w2,
           swa_sinks, hgrn_lb_raw, hgrn_norm_g, ple_w, ple_gate_w, ple_gate_b):
    B, S, D = x.shape
    depth = w_in.shape[0]
    T = B * S
    alpha = (2.0 * depth) ** 0.25
    cosf, sinf = _rope_tables(S)
    ovl = _overlap_t(S)
    amat, pmask = _hgrn_tables()
    dup = lambda w: jnp.concatenate([w, w], axis=2).astype(BF16)
    bf = lambda w: w.astype(BF16)
    wgu, wd, wo, pw, gw = bf(ffn_w_gu), bf(ffn_w_down), bf(w_out), bf(ple_w), bf(ple_gate_w)
    wk1, wk2, wv1, wv2 = bf(cmp_k_w1), dup(cmp_k_w2), bf(cmp_v_w1), dup(cmp_v_w2)
    w_proj = _pack_w_in(w_in)
    lng, lnb = ln_g.reshape(depth, 3, 1, D), ln_b.reshape(depth, 3, 1, D)
    gate_b = ple_gate_b.reshape(depth, 1, D)
    norm_g = hgrn_norm_g.reshape(depth, 1, -1)
    pos2 = cmp_pos.reshape(depth, 2, CMP_STRIDE * HEAD_DIM)
    p3 = p.reshape(depth, T, -1)
    n16 = S // CMP_STRIDE
    xt = x.reshape(T, D)
    for i in range(depth):
        xt = _ffn_ln(xt, _pick(wgu, i, 0), _pick(wd, i, 0), _pick(lng, i, 0), _pick(lnb, i, 0), alpha)
        a, hf, kcm, vcm, gates = _proj_in(xt, _pick(w_proj, i), cosf, sinf, S)
        kc, vc = _compress(kcm.reshape(B, n16, CMP_STRIDE * HEAD_DIM), vcm.reshape(B, n16, CMP_STRIDE * HEAD_DIM),
                           _pick(pos2, i), _pick(wk1, i), _pick(wk2, i), _pick(wv1, i), _pick(wv2, i))
        o_nsa = _nsa_attention(a, gates, kc, vc, ovl, B, S)
        o_swa = _swa_attention(a, swa_sinks, i, B, S)
        o_hgrn = _hgrn(hf, hgrn_lb_raw, norm_g, amat, pmask, i, B, S)
        xt = _ffn_ln(xt, _pick(wgu, i, 1), _pick(wd, i, 1), _pick(lng, i, 2), _pick(lnb, i, 2), alpha,
                     mix=(o_nsa, o_swa, o_hgrn, _pick(wo, i), _pick(lng, i, 1), _pick(lnb, i, 1)),
                     ple=(_pick(p3, i), _pick(pw, i), _pick(gw, i), _pick(gate_b, i)))
    return xt.reshape(B, S, D)
```

```python
import functools

import numpy as np
import jax
import jax.numpy as jnp
from jax import lax
from jax.experimental import pallas as pl
from jax.experimental.pallas import tpu as pltpu

F32 = jnp.float32
BF16 = jnp.bfloat16

HEAD_DIM = 64
ROPE_THETA = 10000.0
LN_EPS = 1e-5
RMS_EPS = 1e-6
NEG_INF = -1e30
LOG_FLOOR = 1e-30
NSA_HEADS = 4
CMP_LEN = 32
CMP_STRIDE = 16
SEL_BLOCK = 64
SEL_TOPN = 16
SEL_FORCE = 1e4
NSA_WINDOW = 512
SWA_HEADS = 4
SWA_KV = 2
SWA_WINDOW = 128
HGRN_HEADS = 4
HGRN_DIM = 128
HGRN_CHUNK = 64

LANES = 128
VMEM_LIMIT = 56 * 1024 * 1024
MASK_BIG = 2.0 ** 100
ATTN_TQ = 128
NSA_TQ = 256
SEL_CK = 512

NT_DIMS = (((1,), (1,)), ((), ()))
TN_DIMS = (((0,), (0,)), ((), ()))


def _dot(a, b):
    return jnp.dot(a, b, preferred_element_type=F32)


def _dot_nt(a, b):
    return lax.dot_general(a, b, NT_DIMS, preferred_element_type=F32)


def _dot_tn(a, b):
    return lax.dot_general(a, b, TN_DIMS, preferred_element_type=F32)


def _split3(x):
    hi = x.astype(BF16)
    r1 = x - hi.astype(F32)
    mid = r1.astype(BF16)
    lo = (r1 - mid.astype(F32)).astype(BF16)
    return hi, mid, lo


def _layer_norm(y, g, b):
    mu = jnp.mean(y, axis=-1, keepdims=True)
    d = y - mu
    var = jnp.mean(d * d, axis=-1, keepdims=True)
    return d * lax.rsqrt(var + LN_EPS) * g + b


def _pick(arr, *lead):
    return arr, lead


def _resident(op):
    arr, lead = op
    tail = arr.shape[len(lead):]
    return pl.BlockSpec((None,) * len(lead) + tail, lambda *_: lead + (0,) * len(tail), pipeline_mode=pl.Buffered(1))


def _rows(op, tm):
    arr, lead = op
    return pl.BlockSpec((None,) * len(lead) + (tm, arr.shape[-1]), lambda i, *_: lead + (i, 0))


def _params(sem):
    return pltpu.CompilerParams(dimension_semantics=sem, vmem_limit_bytes=VMEM_LIMIT)


def _ffn_body(*refs, d_ff, chunk, alpha, mixed):
    if mixed:
        (x_ref, on_ref, os_ref, oh_ref, wo_ref, g0_ref, b0_ref, wgu_ref, wd_ref, g_ref, b_ref,
         p_ref, pw_ref, gw_ref, gb_ref, o_ref) = refs
        mix = jnp.concatenate([on_ref[...], os_ref[...], oh_ref[...]], axis=1)
        x = _layer_norm(alpha * x_ref[...] + _dot(mix, wo_ref[...]), g0_ref[...], b0_ref[...])
    else:
        x_ref, wgu_ref, wd_ref, g_ref, b_ref, o_ref = refs
        x = x_ref[...]
    xb = x.astype(BF16)
    acc = jnp.zeros(x.shape, F32)
    for j in range(d_ff // chunk):
        a = _dot(xb, wgu_ref[:, j * chunk:(j + 1) * chunk])
        u = _dot(xb, wgu_ref[:, d_ff + j * chunk:d_ff + (j + 1) * chunk])
        h = (a * jax.nn.sigmoid(a) * u).astype(BF16)
        acc = acc + _dot(h, wd_ref[j * chunk:(j + 1) * chunk, :])
    y = _layer_norm(alpha * x + 0.5 * acc, g_ref[...], b_ref[...])
    if mixed:
        gate = jax.nn.sigmoid(_dot(y.astype(BF16), gw_ref[...]) + gb_ref[...])
        y = y + gate * _dot(p_ref[...].astype(BF16), pw_ref[...])
    o_ref[...] = y


def _ffn_ln(x, wgu, wd, g, b, alpha, mix=None, ple=None, tm=512, chunk=256):
    T, D = x.shape
    d_ff = wd[0].shape[-2]
    ops = [(_pick(x), _rows)]
    if mix is not None:
        o_nsa, o_swa, o_hgrn, wo, g0, b0 = mix
        ops += [(_pick(o_nsa), _rows), (_pick(o_swa), _rows), (_pick(o_hgrn), _rows),
                (wo, _resident), (g0, _resident), (b0, _resident)]
    ops += [(wgu, _resident), (wd, _resident), (g, _resident), (b, _resident)]
    if mix is not None:
        p, pw, gw, gb = ple
        ops += [(p, _rows), (pw, _resident), (gw, _resident), (gb, _resident)]
    return pl.pallas_call(
        functools.partial(_ffn_body, d_ff=d_ff, chunk=chunk, alpha=alpha, mixed=mix is not None),
        grid=(T // tm,),
        in_specs=[spec(op, tm) if spec is _rows else spec(op) for op, spec in ops],
        out_specs=pl.BlockSpec((tm, D), lambda i: (i, 0)),
        out_shape=jax.ShapeDtypeStruct((T, D), F32),
        compiler_params=_params(("parallel",)),
        name="mix_ffn_ln_ple" if mix is not None else "ffn_ln",
    )(*[op[0] for op, _ in ops])


A_NSA_Q, A_NSA_QR, A_SWA_QR = 0, 256, 512
A_KA, A_KB, A_VS, A_KW, A_VW = 768, 896, 1024, 1152, 1280
A_SK0, A_SK1, A_SV0, A_SV1 = 1408, 1536, 1664, 1792
A_WIDTH = 1920
P_HGRN = 1280
P_WIDTH = P_HGRN + 2048


def _rope(x, cosf, sinf, first_half):
    rot = jnp.where(first_half, pltpu.roll(x, 96, 1), pltpu.roll(x, 32, 1))
    return x * cosf + rot * sinf


def _proj_body(x_ref, w_ref, cos_ref, sin_ref, a_ref, h_ref, kc_ref, vc_ref, gate_ref, cm_scr, *, tm, seq, scale):
    xb = x_ref[...].astype(BF16)
    cosf = cos_ref[...]
    sinf = sin_ref[...]
    lane = lax.broadcasted_iota(jnp.int32, (tm, LANES), 1)
    first_half = (lane % HEAD_DIM) < (HEAD_DIM // 2)
    low = lane < HEAD_DIM
    pos = (pl.program_id(0) % (seq // tm)) * tm + lax.broadcasted_iota(jnp.int32, (tm, LANES), 0)
    blk = pos // SEL_BLOCK
    swap = lambda t: pltpu.roll(t, HEAD_DIM, 1)
    rope = lambda t: _rope(t, cosf, sinf, first_half)

    def put(c0, val):
        a_ref[:, c0:c0 + LANES] = val.astype(BF16)

    mm = lambda c0, c1: _dot_nt(xb, w_ref[c0:c1, :])
    qq = mm(0, 512)
    for half in range(2):
        nq = qq[:, half * LANES:(half + 1) * LANES]
        put(A_NSA_Q + half * LANES, nq * scale)
        put(A_NSA_QR + half * LANES, rope(nq) * scale)
        put(A_SWA_QR + half * LANES, rope(qq[:, 256 + half * LANES:256 + (half + 1) * LANES]) * scale)
    kv = mm(512, 1024)
    sl, wn, sk, sv = (kv[:, i * LANES:(i + 1) * LANES] for i in range(4))
    sl_r = rope(sl)
    put(A_KA, jnp.where(low, sl_r, jnp.where(lane - HEAD_DIM == blk, 1.0, 0.0)))
    put(A_KB, jnp.where(low, jnp.where(lane == blk, 1.0, 0.0), swap(sl_r)))
    put(A_VS, jnp.where(low, swap(sl), 1.0))
    wn_r = rope(wn)
    put(A_KW, jnp.where(low, wn_r, swap(wn_r)))
    put(A_VW, jnp.where(low, swap(wn), 1.0))
    sk_r = rope(sk)
    sk_s = swap(sk_r)
    put(A_SK0, jnp.where(low, sk_r, sk_s))
    put(A_SK1, jnp.where(low, sk_s, sk_r))
    put(A_SV0, jnp.where(low, sv, 1.0))
    put(A_SV1, jnp.where(low, swap(sv), 1.0))
    cg = mm(1024, 1280)
    cm_scr[...] = cg[:, :LANES]
    groups = tm // CMP_STRIDE
    low_g = lax.broadcasted_iota(jnp.int32, (groups, LANES), 1) < HEAD_DIM
    for j in range(CMP_STRIDE // 2):
        even = cm_scr[pl.ds(2 * j, groups, stride=CMP_STRIDE), :]
        odd = cm_scr[pl.ds(2 * j + 1, groups, stride=CMP_STRIDE), :]
        kc_ref[:, j * LANES:(j + 1) * LANES] = jnp.where(low_g, even, swap(odd))
        vc_ref[:, j * LANES:(j + 1) * LANES] = jnp.where(low_g, swap(even), odd)
    gate_ref[...] = jax.nn.sigmoid(cg[:, LANES:])
    for j in range(4):
        h_ref[:, j * 512:(j + 1) * 512] = mm(P_HGRN + j * 512, P_HGRN + (j + 1) * 512)


def _proj_in(x, w, cosf, sinf, seq, tm=512):
    T, D = x.shape
    nseq = seq // tm
    row = lambda w_: pl.BlockSpec((tm, w_), lambda i: (i, 0))
    tab = pl.BlockSpec((tm, LANES), lambda i: (i % nseq, 0))
    grp = pl.BlockSpec((tm // CMP_STRIDE, CMP_STRIDE * HEAD_DIM), lambda i: (i, 0))
    return pl.pallas_call(
        functools.partial(_proj_body, tm=tm, seq=seq, scale=HEAD_DIM ** -0.5),
        grid=(T // tm,),
        in_specs=[row(D), _resident(w), tab, tab],
        out_specs=[row(A_WIDTH), row(2048), grp, grp, row(LANES)],
        out_shape=[jax.ShapeDtypeStruct((T, A_WIDTH), BF16), jax.ShapeDtypeStruct((T, 2048), F32),
                   jax.ShapeDtypeStruct((T // CMP_STRIDE, CMP_STRIDE * HEAD_DIM), F32),
                   jax.ShapeDtypeStruct((T // CMP_STRIDE, CMP_STRIDE * HEAD_DIM), F32),
                   jax.ShapeDtypeStruct((T, LANES), F32)],
        scratch_shapes=[pltpu.VMEM((tm, LANES), F32)],
        compiler_params=_params(("parallel",)),
        name="proj_in",
    )(x, w[0], cosf, sinf)


def _pack_w_in(w_in):
    wt = jnp.swapaxes(w_in, 1, 2)
    offs = np.cumsum([0, 256, 64, 64, 64, 64, 64, 64, 12, 256, 128, 128, 512, 512, 512, 512])
    (nq, kcm, vcm, ksl, vsl, kwn, vwn, ngl, sq, sk, sv, hq, hf, hi, hg) = [
        wt[:, offs[i]:offs[i + 1], :] for i in range(15)]
    pad = jnp.zeros((wt.shape[0], LANES - 12, wt.shape[2]), wt.dtype)
    w = jnp.concatenate([nq, sq, ksl, vsl, kwn, vwn, sk, sv, kcm, vcm, ngl, pad, hq, hf, hi, hg], axis=1).astype(BF16)
    assert w.shape[1] == P_WIDTH
    return w


def _cmp_body(ak_ref, av_ref, pos_ref, wk1_ref, wk2_ref, wv1_ref, wv2_ref, kc_ref, vc_ref):
    half = pos_ref.shape[1]
    for a_ref, w1_ref, w2_ref, o_ref in ((ak_ref, wk1_ref, wk2_ref, kc_ref), (av_ref, wv1_ref, wv2_ref, vc_ref)):
        a = a_ref[0]
        n = a.shape[0]
        p_lo = _dot((a + pos_ref[0:1, :]).astype(BF16), w1_ref[0:half, :])
        p_hi = _dot((a + pos_ref[1:2, :]).astype(BF16), w1_ref[half:2 * half, :])
        pre = p_lo + pltpu.roll(p_hi, n - 1, 0)
        act = (pre * jax.nn.sigmoid(pre)).astype(BF16)
        out = _dot(act, w2_ref[...])
        if o_ref is vc_ref:
            lane = lax.broadcasted_iota(jnp.int32, out.shape, 1)
            out = jnp.where(lane < HEAD_DIM, out, 1.0)
        o_ref[0] = out.astype(BF16)


def _compress(ak, av, pos2, wk1, wk2, wv1, wv2):
    B, n, w = ak.shape
    blk = pl.BlockSpec((1, n, w), lambda b: (b, 0, 0))
    out = pl.BlockSpec((1, n, LANES), lambda b: (b, 0, 0))
    return pl.pallas_call(
        _cmp_body,
        grid=(B,),
        in_specs=[blk, blk] + [_resident(op) for op in (pos2, wk1, wk2, wv1, wv2)],
        out_specs=[out, out],
        out_shape=[jax.ShapeDtypeStruct((B, n, LANES), BF16)] * 2,
        compiler_params=_params(("parallel",)),
        name="nsa_compress",
    )(ak, av, pos2[0], wk1[0], wk2[0], wv1[0], wv2[0])


def _stack_heads(q2, fill, low):
    c0, c1 = q2[:, :LANES], q2[:, LANES:]
    even = jnp.concatenate([jnp.where(low, c0, fill), jnp.where(low, c1, fill)], axis=0)
    odd = jnp.concatenate([jnp.where(low, fill, c0), jnp.where(low, fill, c1)], axis=0)
    return even, odd


def _band_mask(t, kstart, span, window):
    diff = t - (kstart + lax.broadcasted_iota(jnp.int32, (1, span), 1))
    return lax.bitcast_convert_type(diff, jnp.uint32) < jnp.uint32(window)


def _lane_tile_max(s):
    m = s[:, :LANES]
    for k in range(1, s.shape[1] // LANES):
        m = jnp.maximum(m, s[:, k * LANES:(k + 1) * LANES])
    return m


HEAD_BLOCK = ((0, 0), (1, 2), (2, 1), (3, 3))


def _normalize(acc):
    return acc / pltpu.roll(acc, HEAD_DIM, 1)


def _nsa_body(q_ref, qr_ref, gate_ref, kc_ref, vc_ref, ovl_ref, ka_ref, kb_ref, vs_ref, kw_ref, vw_ref, o_ref,
              s_scr, m_scr, acc_scr, *, seq, top_n):
    TQ = NSA_TQ
    n_sel = seq // SEL_BLOCK
    n_cmp = kc_ref.shape[1]
    start = pl.program_id(1) * TQ
    lane = lax.broadcasted_iota(jnp.int32, (TQ, LANES), 1)
    low = lane < HEAD_DIM
    zero = jnp.zeros((TQ, LANES), BF16)
    t1 = start + lax.broadcasted_iota(jnp.int32, (TQ, 1), 0)
    head_rows = [slice(r * TQ, (r + 1) * TQ) for r in range(NSA_HEADS)]
    row_max = lambda s: jnp.max(_lane_tile_max(s), axis=-1, keepdims=True)

    qe, qo = _stack_heads(q_ref[...], zero, low)
    sc = _dot_nt(jnp.concatenate([qe, qo], axis=0), kc_ref[0])
    qr = qr_ref[...]
    span = NSA_WINDOW + TQ
    kstart = pl.multiple_of(jnp.maximum(start - NSA_WINDOW, 0), TQ)
    qwe, qwo = _stack_heads(qr, zero, low)
    sw = _dot_nt(jnp.concatenate([qwe, qwo], axis=0), kw_ref[pl.ds(kstart, span), :])

    cend = lax.broadcasted_iota(jnp.int32, (1, n_cmp), 1) * CMP_STRIDE + (CMP_LEN - 1)
    cval = cend <= t1
    has_cmp = t1 >= CMP_LEN - 1
    scs = [jnp.where(cval, sc[r], NEG_INF) for r in head_rows]
    mcs = [row_max(s) for s in scs]
    ecs = [jnp.exp(s - m) for s, m in zip(scs, mcs)]
    pcs = [jnp.where(has_cmp, e / jnp.sum(e, axis=-1, keepdims=True), 0.0) for e in ecs]
    o_cmp = _dot(jnp.concatenate(pcs, axis=0).astype(BF16), vc_ref[0])
    ovl = ovl_ref[...]
    imp_t = sum(_dot_nt(ovl, piece) for piece in _split3(pcs[0] + pcs[1] + pcs[2] + pcs[3]))

    wmask = _band_mask(t1, kstart, span, NSA_WINDOW)
    window_out = []

    def window_pieces():
        sws = []
        for r in head_rows:
            sws.append(jnp.where(wmask, sw[r], NEG_INF))
            yield
        mws = []
        for s in sws:
            mws.append(jnp.broadcast_to(row_max(s), (TQ, LANES)))
            yield
        pws = []
        for s, m in zip(sws, mws):
            pws.append(jnp.concatenate([jnp.exp(s[:, k * LANES:(k + 1) * LANES] - m)
                                        for k in range(span // LANES)], axis=1).astype(BF16))
            yield
        window_out.append(_normalize(_dot(jnp.concatenate(pws, axis=0), vw_ref[pl.ds(kstart, span), :])))

    pieces = window_pieces()

    gates = gate_ref[...]
    gate_b = []

    def gate_broadcasts():
        for c in range(3 * NSA_HEADS):
            gate_b.append(jnp.broadcast_to(gates[:, c:c + 1], (TQ, LANES)))
            yield

    gate_pieces = gate_broadcasts()

    j = lax.broadcasted_iota(jnp.int32, (n_sel, TQ), 0)
    blk_t = (start + lax.broadcasted_iota(jnp.int32, (n_sel, TQ), 1)) // SEL_BLOCK
    forced = (j == 0) | (j == blk_t) | (j == blk_t - 1)
    score = jnp.where(forced, SEL_FORCE, jnp.where(j <= blk_t, imp_t, -SEL_FORCE))
    groups = [score[8 * g:8 * g + 8, :] for g in range(n_sel // 8)]
    ranks = [jnp.zeros((8, TQ), F32) for _ in groups]
    sub = lax.broadcasted_iota(jnp.int32, (8, TQ), 0)
    for jp in range(n_sel):
        row = score[jp:jp + 1, :]
        for g, sg in enumerate(groups):
            if g > jp // 8:
                ranks[g] = ranks[g] + jnp.where(row >= sg, 1.0, 0.0)
            elif g < jp // 8:
                ranks[g] = ranks[g] + jnp.where(row > sg, 1.0, 0.0)
            else:
                tie = jnp.where(sub > jp % 8, 1.0, 0.0)
                ranks[g] = ranks[g] + jnp.where(row > sg, 1.0, 0.0) + jnp.where(row == sg, tie, 0.0)
        if jp % 4 == 3:
            next(pieces, None)
        elif jp % 4 == 1:
            next(gate_pieces, None)
    for _ in pieces:
        pass
    for _ in gate_pieces:
        pass
    o_win = window_out[0]
    partial = [gate_b[3 * h] * o_cmp[head_rows[r]] + gate_b[3 * h + 2] * o_win[head_rows[r]] for h, r in HEAD_BLOCK]
    bias_t = jnp.where(jnp.concatenate(ranks, axis=0) < top_n, 0.0, -MASK_BIG)
    mdup = jnp.concatenate([bias_t] * (LANES // n_sel), axis=0).T.astype(BF16)

    qse, qso = _stack_heads(qr, mdup, low)
    n_pairs = (start + TQ + 2 * SEL_CK - 1) // (2 * SEL_CK)
    m_scr[...] = jnp.full(m_scr.shape, NEG_INF, F32)
    acc_scr[...] = jnp.zeros(acc_scr.shape, F32)

    def scores(cp, carry):
        raw, tile_max = [], []
        for u in range(2):
            ks = pl.multiple_of((2 * cp + u) * SEL_CK, SEL_CK)
            raw.append((ks, _dot_nt(qse, ka_ref[pl.ds(ks, SEL_CK), :]), _dot_nt(qso, kb_ref[pl.ds(ks, SEL_CK), :])))
        for u, (ks, se, so) in enumerate(raw):
            causal = ks + lax.broadcasted_iota(jnp.int32, (1, SEL_CK), 1) <= t1
            masked = [jnp.where(causal, blk, NEG_INF) for blk in (se[:TQ], se[TQ:], so[:TQ], so[TQ:])]
            s_scr[2 * cp + u] = jnp.concatenate(masked, axis=0)
            tile_max.append(jnp.concatenate([_lane_tile_max(blk) for blk in masked], axis=0))
        pair_max = jnp.max(jnp.maximum(tile_max[0], tile_max[1]), axis=-1, keepdims=True)
        m_scr[...] = jnp.maximum(m_scr[...], jnp.broadcast_to(pair_max, m_scr.shape))
        return carry

    lax.fori_loop(0, n_pairs, scores, 0)
    m_sel = m_scr[...]

    def weighted(cp, carry):
        ks = pl.multiple_of(cp * 2 * SEL_CK, 2 * SEL_CK)
        p = jnp.concatenate([jnp.exp(s_scr[2 * cp + u, :, k * LANES:(k + 1) * LANES] - m_sel)
                             for u in range(2) for k in range(SEL_CK // LANES)], axis=1)
        acc_scr[...] += _dot(p.astype(BF16), vs_ref[pl.ds(ks, 2 * SEL_CK), :])
        return carry

    lax.fori_loop(0, n_pairs, weighted, 0)
    o_slc = _normalize(acc_scr[...])

    heads = [partial[h] + gate_b[3 * h + 1] * o_slc[head_rows[r]] for h, r in HEAD_BLOCK]
    o_ref[...] = jnp.concatenate([jnp.where(low, heads[0], pltpu.roll(heads[1], HEAD_DIM, 1)),
                                  jnp.where(low, heads[2], pltpu.roll(heads[3], HEAD_DIM, 1))],
                                 axis=1).astype(o_ref.dtype)


def _nsa_attention(a, gates, kc, vc, ovl, batch, seq):
    T = a.shape[0]
    nq = seq // NSA_TQ
    n_cmp = kc.shape[1]
    n_sel = seq // SEL_BLOCK
    assert LANES % n_sel == 0 and n_sel <= HEAD_DIM and seq >= NSA_WINDOW + NSA_TQ and seq % (2 * SEL_CK) == 0
    qspec = lambda cb: pl.BlockSpec((NSA_TQ, 256), lambda b, i: (b * nq + i, cb))
    kspec = lambda c0: pl.BlockSpec((seq, LANES), lambda b, i: (b, c0 // LANES))
    cspec = pl.BlockSpec((1, n_cmp, LANES), lambda b, i: (b, 0, 0))
    return pl.pallas_call(
        functools.partial(_nsa_body, seq=seq, top_n=min(SEL_TOPN, n_sel)),
        grid=(batch, nq),
        in_specs=[qspec(A_NSA_Q // 256), qspec(A_NSA_QR // 256),
                  pl.BlockSpec((NSA_TQ, LANES), lambda b, i: (b * nq + i, 0)),
                  cspec, cspec, pl.BlockSpec(ovl.shape, lambda b, i: (0, 0)),
                  kspec(A_KA), kspec(A_KB), kspec(A_VS), kspec(A_KW), kspec(A_VW)],
        out_specs=pl.BlockSpec((NSA_TQ, 256), lambda b, i: (b * nq + i, 0)),
        out_shape=jax.ShapeDtypeStruct((T, 256), BF16),
        scratch_shapes=[pltpu.VMEM((seq // SEL_CK, 4 * NSA_TQ, SEL_CK), F32),
                        pltpu.VMEM((4 * NSA_TQ, LANES), F32), pltpu.VMEM((4 * NSA_TQ, LANES), F32)],
        compiler_params=_params(("parallel", "arbitrary")),
        name="nsa_attention",
    )(a, a, gates, kc, vc, ovl, a, a, a, a, a)


def _overlap_t(seq):
    n_pad = seq // CMP_STRIDE
    n_cmp = (seq - CMP_LEN) // CMP_STRIDE + 1
    ci = np.arange(n_pad)[None, :] * CMP_STRIDE
    sj = np.arange(seq // SEL_BLOCK)[:, None] * SEL_BLOCK
    ovl = (ci < sj + SEL_BLOCK) & (ci + CMP_LEN > sj) & (np.arange(n_pad)[None, :] < n_cmp)
    return jnp.asarray(ovl.astype(np.float32), dtype=BF16)


SWA_STEP = 512


def _swa_body(sink_ref, q_ref, k0_ref, k1_ref, v0_ref, v1_ref, o_ref, *, layer):
    TQ = ATTN_TQ
    span = SWA_WINDOW + TQ
    lane = lax.broadcasted_iota(jnp.int32, (TQ, LANES), 1)
    low = lane < HEAD_DIM
    zero = jnp.zeros((TQ, LANES), BF16)
    kv_refs = ((k0_ref, v0_ref), (k1_ref, v1_ref))
    sinks = [jnp.concatenate([jnp.full((TQ, LANES), sink_ref[layer, 2 * kv], F32),
                              jnp.full((TQ, LANES), sink_ref[layer, 2 * kv + 1], F32)], axis=0)
             for kv in range(SWA_KV)]
    kstarts, scores = [], []
    for sb in range(SWA_STEP // TQ):
        start = pl.program_id(1) * SWA_STEP + sb * TQ
        kstart = pl.multiple_of(jnp.maximum(start - SWA_WINDOW, 0), TQ)
        t1 = start + lax.broadcasted_iota(jnp.int32, (TQ, 1), 0)
        mask = _band_mask(jnp.concatenate([t1, t1], axis=0), kstart, span, SWA_WINDOW)
        q = q_ref[sb * TQ:(sb + 1) * TQ, :]
        for kv in range(SWA_KV):
            c = q[:, kv * LANES:(kv + 1) * LANES]
            qs = jnp.concatenate([jnp.where(low, c, zero), jnp.where(low, zero, c)], axis=0)
            scores.append(jnp.where(mask, _dot_nt(qs, kv_refs[kv][0][pl.ds(kstart, span), :]), NEG_INF))
            kstarts.append(kstart)
    maxes = [jnp.maximum(jnp.broadcast_to(jnp.max(_lane_tile_max(s), axis=-1, keepdims=True), (2 * TQ, LANES)),
                         sinks[n % SWA_KV]) for n, s in enumerate(scores)]
    accs = []
    for n, (s, m) in enumerate(zip(scores, maxes)):
        p = jnp.concatenate([jnp.exp(s[:, k * LANES:(k + 1) * LANES] - m) for k in range(span // LANES)], axis=1)
        accs.append(_dot(p.astype(BF16), kv_refs[n % SWA_KV][1][pl.ds(kstarts[n], span), :]))
    outs = [acc / (pltpu.roll(acc, HEAD_DIM, 1) + jnp.exp(sinks[n % SWA_KV] - m))
            for n, (acc, m) in enumerate(zip(accs, maxes))]
    for sb in range(SWA_STEP // TQ):
        cols = [jnp.where(low, o[:TQ], pltpu.roll(o[TQ:], HEAD_DIM, 1)) for o in outs[SWA_KV * sb:SWA_KV * (sb + 1)]]
        o_ref[sb * TQ:(sb + 1) * TQ, :] = jnp.concatenate(cols, axis=1).astype(o_ref.dtype)


def _swa_attention(a, sinks, layer, batch, seq):
    T = a.shape[0]
    nq = seq // SWA_STEP
    assert seq >= SWA_WINDOW + ATTN_TQ and seq % SWA_STEP == 0
    kspec = lambda c0: pl.BlockSpec((seq, LANES), lambda b, i: (b, c0 // LANES))
    return pl.pallas_call(
        functools.partial(_swa_body, layer=layer),
        grid=(batch, nq),
        in_specs=[pl.BlockSpec(memory_space=pltpu.SMEM),
                  pl.BlockSpec((SWA_STEP, 256), lambda b, i: (b * nq + i, A_SWA_QR // 256)),
                  kspec(A_SK0), kspec(A_SK1), kspec(A_SV0), kspec(A_SV1)],
        out_specs=pl.BlockSpec((SWA_STEP, 256), lambda b, i: (b * nq + i, 0)),
        out_shape=jax.ShapeDtypeStruct((T, 256), BF16),
        compiler_params=_params(("parallel", "arbitrary")),
        name="swa_attention",
    )(sinks, a, a, a, a, a)


HGRN_LEVELS = (32, 16, 8, 4, 2, 1)
HGRN_SUB = 2


def _hgrn_tables():
    C = HGRN_CHUNK
    t = np.arange(C)[:, None]
    u = np.arange(C)[None, :]
    mats, masks = [], []
    for m in HGRN_LEVELS:
        r = (t // (2 * m)) * 2 * m + m - 1
        mats.append(np.where(t > r, (u > r) & (u <= t), (u > t) & (u <= r)))
        masks.append((t // (2 * m)) == (u // (2 * m)))
    mats.append(u <= t)
    mats.append(u > t)
    masks.append(t == u)
    amat = np.concatenate(mats, 0).astype(np.float32)
    return (jnp.asarray(np.concatenate([amat, amat, amat], 1), dtype=BF16),
            jnp.asarray(np.stack(masks).astype(np.float32)))


def _hgrn_body(q_ref, f_ref, i_ref, g_ref, lbraw_ref, ng_ref, amat_ref, pmask_ref, o_ref, state_ref,
               *, layer, rows):
    C, Dh = HGRN_CHUNK, HGRN_DIM
    nlev = len(HGRN_LEVELS)

    @pl.when(pl.program_id(1) == 0)
    def _():
        state_ref[...] = jnp.zeros(state_ref.shape, F32)

    raw = lbraw_ref[...]
    ex = jnp.exp(raw - jnp.max(raw, axis=0, keepdims=True))
    sm = ex / jnp.sum(ex, axis=0, keepdims=True)
    cs = sm[0:1]
    for k in range(1, layer + 1):
        cs = cs + sm[k:k + 1]
    lb = cs - sm[0:1]
    log_lb = jnp.log(jnp.maximum(lb, LOG_FLOOR))
    log_1m = jnp.log1p(-lb)
    W = HGRN_HEADS * Dh
    row = lax.broadcasted_iota(jnp.int32, (C, HGRN_SUB * W), 0)
    amat = amat_ref[...]
    wide = lambda t: jnp.concatenate([t] * HGRN_SUB, axis=1)
    lb_w, log_lb_w, log_1m_w = wide(lb), wide(log_lb), wide(log_1m)
    blocks = [slice(n * Dh, (n + 1) * Dh) for n in range(HGRN_SUB * HGRN_HEADS)]

    def step(ci, carry):
        rws = [pl.ds(pl.multiple_of((ci * HGRN_SUB + u) * C, C), C) for u in range(HGRN_SUB)]
        side = lambda ref: jnp.concatenate([ref[r, :] for r in rws], axis=1)
        z, q, gg = side(f_ref), side(q_ref), side(g_ref)
        v = side(i_ref).astype(BF16)
        e = jnp.exp(-jnp.abs(z))
        inv = 1.0 / (1.0 + e)
        c = log_1m_w + (jnp.minimum(z, 0.0) - jnp.log(1.0 + e))
        lf = jnp.maximum(log_lb_w, c) + jnp.log(1.0 + jnp.exp(-jnp.abs(log_lb_w - c)))
        k = (1.0 - lb_w) * jnp.where(z > 0.0, e * inv, inv)
        dsum = _dot(amat, jnp.concatenate(_split3(lf), axis=0))
        qb, kb = q.astype(BF16), k.astype(BF16)
        atts = [pmask_ref[nlev] * _dot_nt(qb[:, s], kb[:, s]) for s in blocks]
        for lev, m in enumerate(HGRN_LEVELS):
            dec = jnp.exp(dsum[lev * C:(lev + 1) * C])
            if m % 8 == 0:
                runs = [slice(r0, r0 + m) for r0 in range(0, C, m)]
                zeros = jnp.zeros((m, q.shape[1]), F32)
                qt = jnp.concatenate([q[r] * dec[r] if n % 2 else zeros for n, r in enumerate(runs)], axis=0)
                kt = jnp.concatenate([zeros if n % 2 else k[r] * dec[r] for n, r in enumerate(runs)], axis=0)
                qt, kt = qt.astype(BF16), kt.astype(BF16)
            else:
                upper = (row // m) % 2 == 1
                qt = (jnp.where(upper, q, 0.0) * dec).astype(BF16)
                kt = (jnp.where(upper, 0.0, k) * dec).astype(BF16)
            pm = pmask_ref[lev]
            atts = [att + pm * _dot_nt(qt[:, s], kt[:, s]) for att, s in zip(atts, blocks)]
        b = dsum[nlev * C:(nlev + 1) * C]
        rb = dsum[(nlev + 1) * C:(nlev + 2) * C]
        qhat = (q * jnp.exp(b)).astype(BF16)
        khat = (k * jnp.exp(rb)).astype(BF16)
        dlast = jnp.exp(b[C - 1:C, :])
        intra = [_dot(att.astype(BF16), v[:, s]) for att, s in zip(atts, blocks)]
        update = [_dot_tn(v[:, s], khat[:, s]) for s in blocks]
        states = [state_ref[h] for h in range(HGRN_HEADS)]
        outs = []
        for u in range(HGRN_SUB):
            blk = blocks[u * HGRN_HEADS:(u + 1) * HGRN_HEADS]
            outs += [intra[u * HGRN_HEADS + h] + _dot_nt(qhat[:, s], states[h].astype(BF16)) for h, s in enumerate(blk)]
            states = [st * dlast[:, s] + update[u * HGRN_HEADS + h] for h, (s, st) in enumerate(zip(blk, states))]
        for h, st in enumerate(states):
            state_ref[h] = st
        outs = [o * lax.rsqrt(jnp.mean(o * o, axis=-1, keepdims=True) + RMS_EPS) for o in outs]
        gated = jnp.concatenate(outs, axis=1) * wide(ng_ref[...]) * (gg * jax.nn.sigmoid(gg))
        for u, r in enumerate(rws):
            o_ref[r, :] = gated[:, u * W:(u + 1) * W].astype(o_ref.dtype)
        return carry

    lax.fori_loop(0, rows // (C * HGRN_SUB), step, 0)


def _hgrn(hf, lb_raw, norm_g, amat, pmask, layer, batch, seq, rows=512):
    T = hf.shape[0]
    W = HGRN_HEADS * HGRN_DIM
    ns = seq // rows
    spec = lambda cb: pl.BlockSpec((rows, W), lambda b, i: (b * ns + i, cb))
    return pl.pallas_call(
        functools.partial(_hgrn_body, layer=layer, rows=rows),
        grid=(batch, ns),
        in_specs=[spec(0), spec(1), spec(2), spec(3),
                  pl.BlockSpec(lb_raw.shape, lambda b, i: (0, 0)), pl.BlockSpec((None, 1, W), lambda b, i: (layer, 0, 0)),
                  pl.BlockSpec(amat.shape, lambda b, i: (0, 0)), pl.BlockSpec(pmask.shape, lambda b, i: (0, 0, 0))],
        out_specs=pl.BlockSpec((rows, W), lambda b, i: (b * ns + i, 0)),
        out_shape=jax.ShapeDtypeStruct((T, W), BF16),
        scratch_shapes=[pltpu.VMEM((HGRN_HEADS, HGRN_DIM, HGRN_DIM), F32)],
        compiler_params=_params(("parallel", "arbitrary")),
        name="hgrn2",
    )(hf, hf, hf, hf, lb_raw, norm_g, amat, pmask)


def _rope_tables(seq):
    inv = 1.0 / (ROPE_THETA ** (jnp.arange(0, HEAD_DIM, 2, dtype=F32) / HEAD_DIM))
    ang = jnp.arange(seq, dtype=F32)[:, None] * inv[None, :]
    cos, sin = jnp.cos(ang), jnp.sin(ang)
    return jnp.tile(cos, (1, 4)), jnp.concatenate([-sin, sin, -sin, sin], axis=1)


def kernel(x, p, ln_g, ln_b, ffn_w_gu, ffn_w_down, w_in, w_out, cmp_pos, cmp_k_w1, cmp_k_w2, cmp_v_w1, cmp_v_w2,
           swa_sinks, hgrn_lb_raw, hgrn_norm_g, ple_w, ple_gate_w, ple_gate_b):
    B, S, D = x.shape
    depth = w_in.shape[0]
    T = B * S
    alpha = (2.0 * depth) ** 0.25
    cosf, sinf = _rope_tables(S)
    ovl = _overlap_t(S)
    amat, pmask = _hgrn_tables()
    dup = lambda w: jnp.concatenate([w, w], axis=2).astype(BF16)
    bf = lambda w: w.astype(BF16)
    wgu, wd, wo, pw, gw = bf(ffn_w_gu), bf(ffn_w_down), bf(w_out), bf(ple_w), bf(ple_gate_w)
    wk1, wk2, wv1, wv2 = bf(cmp_k_w1), dup(cmp_k_w2), bf(cmp_v_w1), dup(cmp_v_w2)
    w_proj = _pack_w_in(w_in)
    lng, lnb = ln_g.reshape(depth, 3, 1, D), ln_b.reshape(depth, 3, 1, D)
    gate_b = ple_gate_b.reshape(depth, 1, D)
    norm_g = hgrn_norm_g.reshape(depth, 1, -1)
    pos2 = cmp_pos.reshape(depth, 2, CMP_STRIDE * HEAD_DIM)
    p3 = p.reshape(depth, T, -1)
    n16 = S // CMP_STRIDE
    xt = x.reshape(T, D)
    for i in range(depth):
        xt = _ffn_ln(xt, _pick(wgu, i, 0), _pick(wd, i, 0), _pick(lng, i, 0), _pick(lnb, i, 0), alpha)
        a, hf, kcm, vcm, gates = _proj_in(xt, _pick(w_proj, i), cosf, sinf, S)
        kc, vc = _compress(kcm.reshape(B, n16, CMP_STRIDE * HEAD_DIM), vcm.reshape(B, n16, CMP_STRIDE * HEAD_DIM),
                           _pick(pos2, i), _pick(wk1, i), _pick(wk2, i), _pick(wv1, i), _pick(wv2, i))
        o_nsa = _nsa_attention(a, gates, kc, vc, ovl, B, S)
        o_swa = _swa_attention(a, swa_sinks, i, B, S)
        o_hgrn = _hgrn(hf, hgrn_lb_raw, norm_g, amat, pmask, i, B, S)
        xt = _ffn_ln(xt, _pick(wgu, i, 1), _pick(wd, i, 1), _pick(lng, i, 2), _pick(lnb, i, 2), alpha,
                     mix=(o_nsa, o_swa, o_hgrn, _pick(wo, i), _pick(lng, i, 1), _pick(lnb, i, 1)),
                     ple=(_pick(p3, i), _pick(pw, i), _pick(gw, i), _pick(gate_b, i)))
    return xt.reshape(B, S, D)
```

```python
import functools

import numpy as np
import jax
import jax.numpy as jnp
from jax import lax
from jax.experimental import pallas as pl
from jax.experimental.pallas import tpu as pltpu

F32 = jnp.float32
BF16 = jnp.bfloat16

HEAD_DIM = 64
ROPE_THETA = 10000.0
LN_EPS = 1e-5
RMS_EPS = 1e-6
NEG_INF = -1e30
LOG_FLOOR = 1e-30
NSA_HEADS = 4
CMP_LEN = 32
CMP_STRIDE = 16
SEL_BLOCK = 64
SEL_TOPN = 16
SEL_FORCE = 1e4
NSA_WINDOW = 512
SWA_HEADS = 4
SWA_KV = 2
SWA_WINDOW = 128
HGRN_HEADS = 4
HGRN_DIM = 128
HGRN_CHUNK = 64

LANES = 128
VMEM_LIMIT = 56 * 1024 * 1024
MASK_BIG = 2.0 ** 100
ATTN_TQ = 128
NSA_TQ = 256
SEL_CK = 512

NT_DIMS = (((1,), (1,)), ((), ()))
TN_DIMS = (((0,), (0,)), ((), ()))


def _dot(a, b):
    return jnp.dot(a, b, preferred_element_type=F32)


def _dot_nt(a, b):
    return lax.dot_general(a, b, NT_DIMS, preferred_element_type=F32)


def _dot_tn(a, b):
    return lax.dot_general(a, b, TN_DIMS, preferred_element_type=F32)


def _split3(x):
    hi = x.astype(BF16)
    r1 = x - hi.astype(F32)
    mid = r1.astype(BF16)
    lo = (r1 - mid.astype(F32)).astype(BF16)
    return hi, mid, lo


def _layer_norm(y, g, b):
    mu = jnp.mean(y, axis=-1, keepdims=True)
    d = y - mu
    var = jnp.mean(d * d, axis=-1, keepdims=True)
    return d * lax.rsqrt(var + LN_EPS) * g + b


def _pick(arr, *lead):
    return arr, lead


def _resident(op):
    arr, lead = op
    tail = arr.shape[len(lead):]
    return pl.BlockSpec((None,) * len(lead) + tail, lambda *_: lead + (0,) * len(tail), pipeline_mode=pl.Buffered(1))


def _rows(op, tm):
    arr, lead = op
    return pl.BlockSpec((None,) * len(lead) + (tm, arr.shape[-1]), lambda i, *_: lead + (i, 0))


def _params(sem):
    return pltpu.CompilerParams(dimension_semantics=sem, vmem_limit_bytes=VMEM_LIMIT)


def _ffn_body(*refs, d_ff, chunk, alpha, mixed):
    if mixed:
        (x_ref, on_ref, os_ref, oh_ref, wo_ref, g0_ref, b0_ref, wgu_ref, wd_ref, g_ref, b_ref,
         p_ref, pw_ref, gw_ref, gb_ref, o_ref) = refs
        mix = jnp.concatenate([on_ref[...], os_ref[...], oh_ref[...]], axis=1)
        x = _layer_norm(alpha * x_ref[...] + _dot(mix, wo_ref[...]), g0_ref[...], b0_ref[...])
    else:
        x_ref, wgu_ref, wd_ref, g_ref, b_ref, o_ref = refs
        x = x_ref[...]
    xb = x.astype(BF16)
    acc = jnp.zeros(x.shape, F32)
    for j in range(d_ff // chunk):
        a = _dot(xb, wgu_ref[:, j * chunk:(j + 1) * chunk])
        u = _dot(xb, wgu_ref[:, d_ff + j * chunk:d_ff + (j + 1) * chunk])
        h = (a * jax.nn.sigmoid(a) * u).astype(BF16)
        acc = acc + _dot(h, wd_ref[j * chunk:(j + 1) * chunk, :])
    y = _layer_norm(alpha * x + 0.5 * acc, g_ref[...], b_ref[...])
    if mixed:
        gate = jax.nn.sigmoid(_dot(y.astype(BF16), gw_ref[...]) + gb_ref[...])
        y = y + gate * _dot(p_ref[...].astype(BF16), pw_ref[...])
    o_ref[...] = y


def _ffn_ln(x, wgu, wd, g, b, alpha, mix=None, ple=None, tm=512, chunk=256):
    T, D = x.shape
    d_ff = wd[0].shape[-2]
    ops = [(_pick(x), _rows)]
    if mix is not None:
        o_nsa, o_swa, o_hgrn, wo, g0, b0 = mix
        ops += [(_pick(o_nsa), _rows), (_pick(o_swa), _rows), (_pick(o_hgrn), _rows),
                (wo, _resident), (g0, _resident), (b0, _resident)]
    ops += [(wgu, _resident), (wd, _resident), (g, _resident), (b, _resident)]
    if mix is not None:
        p, pw, gw, gb = ple
        ops += [(p, _rows), (pw, _resident), (gw, _resident), (gb, _resident)]
    return pl.pallas_call(
        functools.partial(_ffn_body, d_ff=d_ff, chunk=chunk, alpha=alpha, mixed=mix is not None),
        grid=(T // tm,),
        in_specs=[spec(op, tm) if spec is _rows else spec(op) for op, spec in ops],
        out_specs=pl.BlockSpec((tm, D), lambda i: (i, 0)),
        out_shape=jax.ShapeDtypeStruct((T, D), F32),
        compiler_params=_params(("parallel",)),
        name="mix_ffn_ln_ple" if mix is not None else "ffn_ln",
    )(*[op[0] for op, _ in ops])


A_NSA_Q, A_NSA_QR, A_SWA_QR = 0, 256, 512
A_KA, A_KB, A_VS, A_KW, A_VW = 768, 896, 1024, 1152, 1280
A_SK0, A_SK1, A_SV0, A_SV1 = 1408, 1536, 1664, 1792
A_WIDTH = 1920
P_HGRN = 1280
P_WIDTH = P_HGRN + 2048


def _rope(x, cosf, sinf, first_half):
    rot = jnp.where(first_half, pltpu.roll(x, 96, 1), pltpu.roll(x, 32, 1))
    return x * cosf + rot * sinf


def _proj_body(x_ref, w_ref, cos_ref, sin_ref, a_ref, h_ref, kc_ref, vc_ref, gate_ref, cm_scr, *, tm, seq, scale):
    xb = x_ref[...].astype(BF16)
    cosf = cos_ref[...]
    sinf = sin_ref[...]
    lane = lax.broadcasted_iota(jnp.int32, (tm, LANES), 1)
    first_half = (lane % HEAD_DIM) < (HEAD_DIM // 2)
    low = lane < HEAD_DIM
    pos = (pl.program_id(0) % (seq // tm)) * tm + lax.broadcasted_iota(jnp.int32, (tm, LANES), 0)
    blk = pos // SEL_BLOCK
    swap = lambda t: pltpu.roll(t, HEAD_DIM, 1)
    rope = lambda t: _rope(t, cosf, sinf, first_half)

    def put(c0, val):
        a_ref[:, c0:c0 + LANES] = val.astype(BF16)

    mm = lambda c0, c1: _dot_nt(xb, w_ref[c0:c1, :])
    qq = mm(0, 512)
    for half in range(2):
        nq = qq[:, half * LANES:(half + 1) * LANES]
        put(A_NSA_Q + half * LANES, nq * scale)
        put(A_NSA_QR + half * LANES, rope(nq) * scale)
        put(A_SWA_QR + half * LANES, rope(qq[:, 256 + half * LANES:256 + (half + 1) * LANES]) * scale)
    kv = mm(512, 1024)
    sl, wn, sk, sv = (kv[:, i * LANES:(i + 1) * LANES] for i in range(4))
    sl_r = rope(sl)
    put(A_KA, jnp.where(low, sl_r, jnp.where(lane - HEAD_DIM == blk, 1.0, 0.0)))
    put(A_KB, jnp.where(low, jnp.where(lane == blk, 1.0, 0.0), swap(sl_r)))
    put(A_VS, jnp.where(low, swap(sl), 1.0))
    wn_r = rope(wn)
    put(A_KW, jnp.where(low, wn_r, swap(wn_r)))
    put(A_VW, jnp.where(low, swap(wn), 1.0))
    sk_r = rope(sk)
    sk_s = swap(sk_r)
    put(A_SK0, jnp.where(low, sk_r, sk_s))
    put(A_SK1, jnp.where(low, sk_s, sk_r))
    put(A_SV0, jnp.where(low, sv, 1.0))
    put(A_SV1, jnp.where(low, swap(sv), 1.0))
    cg = mm(1024, 1280)
    cm_scr[...] = cg[:, :LANES]
    groups = tm // CMP_STRIDE
    low_g = lax.broadcasted_iota(jnp.int32, (groups, LANES), 1) < HEAD_DIM
    for j in range(CMP_STRIDE // 2):
        even = cm_scr[pl.ds(2 * j, groups, stride=CMP_STRIDE), :]
        odd = cm_scr[pl.ds(2 * j + 1, groups, stride=CMP_STRIDE), :]
        kc_ref[:, j * LANES:(j + 1) * LANES] = jnp.where(low_g, even, swap(odd))
        vc_ref[:, j * LANES:(j + 1) * LANES] = jnp.where(low_g, swap(even), odd)
    gate_ref[...] = jax.nn.sigmoid(cg[:, LANES:])
    for j in range(4):
        h_ref[:, j * 512:(j + 1) * 512] = mm(P_HGRN + j * 512, P_HGRN + (j + 1) * 512)


def _proj_in(x, w, cosf, sinf, seq, tm=512):
    T, D = x.shape
    nseq = seq // tm
    row = lambda w_: pl.BlockSpec((tm, w_), lambda i: (i, 0))
    tab = pl.BlockSpec((tm, LANES), lambda i: (i % nseq, 0))
    grp = pl.BlockSpec((tm // CMP_STRIDE, CMP_STRIDE * HEAD_DIM), lambda i: (i, 0))
    return pl.pallas_call(
        functools.partial(_proj_body, tm=tm, seq=seq, scale=HEAD_DIM ** -0.5),
        grid=(T // tm,),
        in_specs=[row(D), _resident(w), tab, tab],
        out_specs=[row(A_WIDTH), row(2048), grp, grp, row(LANES)],
        out_shape=[jax.ShapeDtypeStruct((T, A_WIDTH), BF16), jax.ShapeDtypeStruct((T, 2048), F32),
                   jax.ShapeDtypeStruct((T // CMP_STRIDE, CMP_STRIDE * HEAD_DIM), F32),
                   jax.ShapeDtypeStruct((T // CMP_STRIDE, CMP_STRIDE * HEAD_DIM), F32),
                   jax.ShapeDtypeStruct((T, LANES), F32)],
        scratch_shapes=[pltpu.VMEM((tm, LANES), F32)],
        compiler_params=_params(("parallel",)),
        name="proj_in",
    )(x, w[0], cosf, sinf)


def _pack_w_in(w_in):
    wt = jnp.swapaxes(w_in, 1, 2)
    offs = np.cumsum([0, 256, 64, 64, 64, 64, 64, 64, 12, 256, 128, 128, 512, 512, 512, 512])
    (nq, kcm, vcm, ksl, vsl, kwn, vwn, ngl, sq, sk, sv, hq, hf, hi, hg) = [
        wt[:, offs[i]:offs[i + 1], :] for i in range(15)]
    pad = jnp.zeros((wt.shape[0], LANES - 12, wt.shape[2]), wt.dtype)
    w = jnp.concatenate([nq, sq, ksl, vsl, kwn, vwn, sk, sv, kcm, vcm, ngl, pad, hq, hf, hi, hg], axis=1).astype(BF16)
    assert w.shape[1] == P_WIDTH
    return w


def _cmp_body(ak_ref, av_ref, pos_ref, wk1_ref, wk2_ref, wv1_ref, wv2_ref, kc_ref, vc_ref):
    half = pos_ref.shape[1]
    for a_ref, w1_ref, w2_ref, o_ref in ((ak_ref, wk1_ref, wk2_ref, kc_ref), (av_ref, wv1_ref, wv2_ref, vc_ref)):
        a = a_ref[0]
        n = a.shape[0]
        p_lo = _dot((a + pos_ref[0:1, :]).astype(BF16), w1_ref[0:half, :])
        p_hi = _dot((a + pos_ref[1:2, :]).astype(BF16), w1_ref[half:2 * half, :])
        pre = p_lo + pltpu.roll(p_hi, n - 1, 0)
        act = (pre * jax.nn.sigmoid(pre)).astype(BF16)
        out = _dot(act, w2_ref[...])
        if o_ref is vc_ref:
            lane = lax.broadcasted_iota(jnp.int32, out.shape, 1)
            out = jnp.where(lane < HEAD_DIM, out, 1.0)
        o_ref[0] = out.astype(BF16)


def _compress(ak, av, pos2, wk1, wk2, wv1, wv2):
    B, n, w = ak.shape
    blk = pl.BlockSpec((1, n, w), lambda b: (b, 0, 0))
    out = pl.BlockSpec((1, n, LANES), lambda b: (b, 0, 0))
    return pl.pallas_call(
        _cmp_body,
        grid=(B,),
        in_specs=[blk, blk] + [_resident(op) for op in (pos2, wk1, wk2, wv1, wv2)],
        out_specs=[out, out],
        out_shape=[jax.ShapeDtypeStruct((B, n, LANES), BF16)] * 2,
        compiler_params=_params(("parallel",)),
        name="nsa_compress",
    )(ak, av, pos2[0], wk1[0], wk2[0], wv1[0], wv2[0])


def _stack_heads(q2, fill, low):
    c0, c1 = q2[:, :LANES], q2[:, LANES:]
    even = jnp.concatenate([jnp.where(low, c0, fill), jnp.where(low, c1, fill)], axis=0)
    odd = jnp.concatenate([jnp.where(low, fill, c0), jnp.where(low, fill, c1)], axis=0)
    return even, odd


def _band_mask(t, kstart, span, window):
    diff = t - (kstart + lax.broadcasted_iota(jnp.int32, (1, span), 1))
    return lax.bitcast_convert_type(diff, jnp.uint32) < jnp.uint32(window)


def _lane_tile_max(s):
    m = s[:, :LANES]
    for k in range(1, s.shape[1] // LANES):
        m = jnp.maximum(m, s[:, k * LANES:(k + 1) * LANES])
    return m


HEAD_BLOCK = ((0, 0), (1, 2), (2, 1), (3, 3))


def _normalize(acc):
    return acc / pltpu.roll(acc, HEAD_DIM, 1)


def _nsa_body(q_ref, qr_ref, gate_ref, kc_ref, vc_ref, ovl_ref, ka_ref, kb_ref, vs_ref, kw_ref, vw_ref, o_ref,
              s_scr, m_scr, acc_scr, *, seq, top_n):
    TQ = NSA_TQ
    n_sel = seq // SEL_BLOCK
    n_cmp = kc_ref.shape[1]
    start = pl.program_id(1) * TQ
    lane = lax.broadcasted_iota(jnp.int32, (TQ, LANES), 1)
    low = lane < HEAD_DIM
    zero = jnp.zeros((TQ, LANES), BF16)
    t1 = start + lax.broadcasted_iota(jnp.int32, (TQ, 1), 0)
    head_rows = [slice(r * TQ, (r + 1) * TQ) for r in range(NSA_HEADS)]
    row_max = lambda s: jnp.max(_lane_tile_max(s), axis=-1, keepdims=True)

    qe, qo = _stack_heads(q_ref[...], zero, low)
    sc = _dot_nt(jnp.concatenate([qe, qo], axis=0), kc_ref[0])
    qr = qr_ref[...]
    span = NSA_WINDOW + TQ
    kstart = pl.multiple_of(jnp.maximum(start - NSA_WINDOW, 0), TQ)
    qwe, qwo = _stack_heads(qr, zero, low)
    sw = _dot_nt(jnp.concatenate([qwe, qwo], axis=0), kw_ref[pl.ds(kstart, span), :])

    cend = lax.broadcasted_iota(jnp.int32, (1, n_cmp), 1) * CMP_STRIDE + (CMP_LEN - 1)
    cval = cend <= t1
    has_cmp = t1 >= CMP_LEN - 1
    scs = [jnp.where(cval, sc[r], NEG_INF) for r in head_rows]
    mcs = [row_max(s) for s in scs]
    ecs = [jnp.exp(s - m) for s, m in zip(scs, mcs)]
    pcs = [jnp.where(has_cmp, e / jnp.sum(e, axis=-1, keepdims=True), 0.0) for e in ecs]
    o_cmp = _dot(jnp.concatenate(pcs, axis=0).astype(BF16), vc_ref[0])
    ovl = ovl_ref[...]
    imp_t = sum(_dot_nt(ovl, piece) for piece in _split3(pcs[0] + pcs[1] + pcs[2] + pcs[3]))

    wmask = _band_mask(t1, kstart, span, NSA_WINDOW)
    window_out = []

    def window_pieces():
        sws = []
        for r in head_rows:
            sws.append(jnp.where(wmask, sw[r], NEG_INF))
            yield
        mws = []
        for s in sws:
            mws.append(jnp.broadcast_to(row_max(s), (TQ, LANES)))
            yield
        pws = []
        for s, m in zip(sws, mws):
            pws.append(jnp.concatenate([jnp.exp(s[:, k * LANES:(k + 1) * LANES] - m)
                                        for k in range(span // LANES)], axis=1).astype(BF16))
            yield
        window_out.append(_normalize(_dot(jnp.concatenate(pws, axis=0), vw_ref[pl.ds(kstart, span), :])))

    pieces = window_pieces()

    gates = gate_ref[...]
    gate_b = []

    def gate_broadcasts():
        for c in range(3 * NSA_HEADS):
            gate_b.append(jnp.broadcast_to(gates[:, c:c + 1], (TQ, LANES)))
            yield

    gate_pieces = gate_broadcasts()

    j = lax.broadcasted_iota(jnp.int32, (n_sel, TQ), 0)
    blk_t = (start + lax.broadcasted_iota(jnp.int32, (n_sel, TQ), 1)) // SEL_BLOCK
    forced = (j == 0) | (j == blk_t) | (j == blk_t - 1)
    score = jnp.where(forced, SEL_FORCE, jnp.where(j <= blk_t, imp_t, -SEL_FORCE))
    groups = [score[8 * g:8 * g + 8, :] for g in range(n_sel // 8)]
    ranks = [jnp.zeros((8, TQ), F32) for _ in groups]
    sub = lax.broadcasted_iota(jnp.int32, (8, TQ), 0)
    for jp in range(n_sel):
        row = score[jp:jp + 1, :]
        for g, sg in enumerate(groups):
            if g > jp // 8:
                ranks[g] = ranks[g] + jnp.where(row >= sg, 1.0, 0.0)
            elif g < jp // 8:
                ranks[g] = ranks[g] + jnp.where(row > sg, 1.0, 0.0)
            else:
                tie = jnp.where(sub > jp % 8, 1.0, 0.0)
                ranks[g] = ranks[g] + jnp.where(row > sg, 1.0, 0.0) + jnp.where(row == sg, tie, 0.0)
        if jp % 4 == 3:
            next(pieces, None)
        elif jp % 4 == 1:
            next(gate_pieces, None)
    for _ in pieces:
        pass
    for _ in gate_pieces:
        pass
    o_win = window_out[0]
    partial = [gate_b[3 * h] * o_cmp[head_rows[r]] + gate_b[3 * h + 2] * o_win[head_rows[r]] for h, r in HEAD_BLOCK]
    bias_t = jnp.where(jnp.concatenate(ranks, axis=0) < top_n, 0.0, -MASK_BIG)
    mdup = jnp.concatenate([bias_t] * (LANES // n_sel), axis=0).T.astype(BF16)

    qse, qso = _stack_heads(qr, mdup, low)
    n_pairs = (start + TQ + 2 * SEL_CK - 1) // (2 * SEL_CK)

    def raw_scores(cp):
        raw = []
        for u in range(2):
            ks = pl.multiple_of((2 * cp + u) * SEL_CK, SEL_CK)
            raw.append((ks, _dot_nt(qse, ka_ref[pl.ds(ks, SEL_CK), :]), _dot_nt(qso, kb_ref[pl.ds(ks, SEL_CK), :])))
        return raw

    def finish_scores(cp, raw):
        tile_max = []
        for u, (ks, se, so) in enumerate(raw):
            causal = ks + lax.broadcasted_iota(jnp.int32, (1, SEL_CK), 1) <= t1
            masked = [jnp.where(causal, blk, NEG_INF) for blk in (se[:TQ], se[TQ:], so[:TQ], so[TQ:])]
            s_scr[2 * cp + u] = jnp.concatenate(masked, axis=0)
            tile_max.append(jnp.concatenate([_lane_tile_max(blk) for blk in masked], axis=0))
        pair_max = jnp.max(jnp.maximum(tile_max[0], tile_max[1]), axis=-1, keepdims=True)
        return jnp.broadcast_to(pair_max, m_scr.shape)

    def weighted(cp, m):
        ks = pl.multiple_of(cp * 2 * SEL_CK, 2 * SEL_CK)
        p = jnp.concatenate([jnp.exp(s_scr[2 * cp + u, :, k * LANES:(k + 1) * LANES] - m)
                             for u in range(2) for k in range(SEL_CK // LANES)], axis=1)
        return _dot(p.astype(BF16), vs_ref[pl.ds(ks, 2 * SEL_CK), :])

    m_scr[...] = finish_scores(0, raw_scores(0))
    acc_scr[...] = jnp.zeros(acc_scr.shape, F32)

    def skewed(cp, carry):
        raw = raw_scores(cp + 1)
        m_old = m_scr[...]
        acc = acc_scr[...] + weighted(cp, m_old)
        m_new = jnp.maximum(m_old, finish_scores(cp + 1, raw))
        acc_scr[...] = acc * jnp.exp(m_old - m_new)
        m_scr[...] = m_new
        return carry

    lax.fori_loop(0, n_pairs - 1, skewed, 0)
    o_slc = _normalize(acc_scr[...] + weighted(n_pairs - 1, m_scr[...]))

    heads = [partial[h] + gate_b[3 * h + 1] * o_slc[head_rows[r]] for h, r in HEAD_BLOCK]
    o_ref[...] = jnp.concatenate([jnp.where(low, heads[0], pltpu.roll(heads[1], HEAD_DIM, 1)),
                                  jnp.where(low, heads[2], pltpu.roll(heads[3], HEAD_DIM, 1))],
                                 axis=1).astype(o_ref.dtype)


def _nsa_attention(a, gates, kc, vc, ovl, batch, seq):
    T = a.shape[0]
    nq = seq // NSA_TQ
    n_cmp = kc.shape[1]
    n_sel = seq // SEL_BLOCK
    assert LANES % n_sel == 0 and n_sel <= HEAD_DIM and seq >= NSA_WINDOW + NSA_TQ and seq % (2 * SEL_CK) == 0
    qspec = lambda cb: pl.BlockSpec((NSA_TQ, 256), lambda b, i: (b * nq + i, cb))
    kspec = lambda c0: pl.BlockSpec((seq, LANES), lambda b, i: (b, c0 // LANES))
    cspec = pl.BlockSpec((1, n_cmp, LANES), lambda b, i: (b, 0, 0))
    return pl.pallas_call(
        functools.partial(_nsa_body, seq=seq, top_n=min(SEL_TOPN, n_sel)),
        grid=(batch, nq),
        in_specs=[qspec(A_NSA_Q // 256), qspec(A_NSA_QR // 256),
                  pl.BlockSpec((NSA_TQ, LANES), lambda b, i: (b * nq + i, 0)),
                  cspec, cspec, pl.BlockSpec(ovl.shape, lambda b, i: (0, 0)),
                  kspec(A_KA), kspec(A_KB), kspec(A_VS), kspec(A_KW), kspec(A_VW)],
        out_specs=pl.BlockSpec((NSA_TQ, 256), lambda b, i: (b * nq + i, 0)),
        out_shape=jax.ShapeDtypeStruct((T, 256), BF16),
        scratch_shapes=[pltpu.VMEM((seq // SEL_CK, 4 * NSA_TQ, SEL_CK), F32),
                        pltpu.VMEM((4 * NSA_TQ, LANES), F32), pltpu.VMEM((4 * NSA_TQ, LANES), F32)],
        compiler_params=_params(("parallel", "arbitrary")),
        name="nsa_attention",
    )(a, a, gates, kc, vc, ovl, a, a, a, a, a)


def _overlap_t(seq):
    n_pad = seq // CMP_STRIDE
    n_cmp = (seq - CMP_LEN) // CMP_STRIDE + 1
    ci = np.arange(n_pad)[None, :] * CMP_STRIDE
    sj = np.arange(seq // SEL_BLOCK)[:, None] * SEL_BLOCK
    ovl = (ci < sj + SEL_BLOCK) & (ci + CMP_LEN > sj) & (np.arange(n_pad)[None, :] < n_cmp)
    return jnp.asarray(ovl.astype(np.float32), dtype=BF16)


SWA_STEP = 512


def _swa_body(sink_ref, q_ref, k0_ref, k1_ref, v0_ref, v1_ref, o_ref, *, layer):
    TQ = ATTN_TQ
    span = SWA_WINDOW + TQ
    lane = lax.broadcasted_iota(jnp.int32, (TQ, LANES), 1)
    low = lane < HEAD_DIM
    zero = jnp.zeros((TQ, LANES), BF16)
    kv_refs = ((k0_ref, v0_ref), (k1_ref, v1_ref))
    sinks = [jnp.concatenate([jnp.full((TQ, LANES), sink_ref[layer, 2 * kv], F32),
                              jnp.full((TQ, LANES), sink_ref[layer, 2 * kv + 1], F32)], axis=0)
             for kv in range(SWA_KV)]
    kstarts, scores = [], []
    for sb in range(SWA_STEP // TQ):
        start = pl.program_id(1) * SWA_STEP + sb * TQ
        kstart = pl.multiple_of(jnp.maximum(start - SWA_WINDOW, 0), TQ)
        t1 = start + lax.broadcasted_iota(jnp.int32, (TQ, 1), 0)
        mask = _band_mask(jnp.concatenate([t1, t1], axis=0), kstart, span, SWA_WINDOW)
        q = q_ref[sb * TQ:(sb + 1) * TQ, :]
        for kv in range(SWA_KV):
            c = q[:, kv * LANES:(kv + 1) * LANES]
            qs = jnp.concatenate([jnp.where(low, c, zero), jnp.where(low, zero, c)], axis=0)
            scores.append(jnp.where(mask, _dot_nt(qs, kv_refs[kv][0][pl.ds(kstart, span), :]), NEG_INF))
            kstarts.append(kstart)
    maxes = [jnp.maximum(jnp.broadcast_to(jnp.max(_lane_tile_max(s), axis=-1, keepdims=True), (2 * TQ, LANES)),
                         sinks[n % SWA_KV]) for n, s in enumerate(scores)]
    accs = []
    for n, (s, m) in enumerate(zip(scores, maxes)):
        p = jnp.concatenate([jnp.exp(s[:, k * LANES:(k + 1) * LANES] - m) for k in range(span // LANES)], axis=1)
        accs.append(_dot(p.astype(BF16), kv_refs[n % SWA_KV][1][pl.ds(kstarts[n], span), :]))
    outs = [acc / (pltpu.roll(acc, HEAD_DIM, 1) + jnp.exp(sinks[n % SWA_KV] - m))
            for n, (acc, m) in enumerate(zip(accs, maxes))]
    for sb in range(SWA_STEP // TQ):
        cols = [jnp.where(low, o[:TQ], pltpu.roll(o[TQ:], HEAD_DIM, 1)) for o in outs[SWA_KV * sb:SWA_KV * (sb + 1)]]
        o_ref[sb * TQ:(sb + 1) * TQ, :] = jnp.concatenate(cols, axis=1).astype(o_ref.dtype)


def _swa_attention(a, sinks, layer, batch, seq):
    T = a.shape[0]
    nq = seq // SWA_STEP
    assert seq >= SWA_WINDOW + ATTN_TQ and seq % SWA_STEP == 0
    kspec = lambda c0: pl.BlockSpec((seq, LANES), lambda b, i: (b, c0 // LANES))
    return pl.pallas_call(
        functools.partial(_swa_body, layer=layer),
        grid=(batch, nq),
        in_specs=[pl.BlockSpec(memory_space=pltpu.SMEM),
                  pl.BlockSpec((SWA_STEP, 256), lambda b, i: (b * nq + i, A_SWA_QR // 256)),
                  kspec(A_SK0), kspec(A_SK1), kspec(A_SV0), kspec(A_SV1)],
        out_specs=pl.BlockSpec((SWA_STEP, 256), lambda b, i: (b * nq + i, 0)),
        out_shape=jax.ShapeDtypeStruct((T, 256), BF16),
        compiler_params=_params(("parallel", "arbitrary")),
        name="swa_attention",
    )(sinks, a, a, a, a, a)


HGRN_LEVELS = (32, 16, 8, 4, 2, 1)
HGRN_SUB = 2


def _hgrn_tables():
    C = HGRN_CHUNK
    t = np.arange(C)[:, None]
    u = np.arange(C)[None, :]
    mats, masks = [], []
    for m in HGRN_LEVELS:
        r = (t // (2 * m)) * 2 * m + m - 1
        mats.append(np.where(t > r, (u > r) & (u <= t), (u > t) & (u <= r)))
        masks.append((t // (2 * m)) == (u // (2 * m)))
    mats.append(u <= t)
    mats.append(u > t)
    masks.append(t == u)
    amat = np.concatenate(mats, 0).astype(np.float32)
    return (jnp.asarray(np.concatenate([amat, amat, amat], 1), dtype=BF16),
            jnp.asarray(np.stack(masks).astype(np.float32)))


def _hgrn_body(q_ref, f_ref, i_ref, g_ref, lbraw_ref, ng_ref, amat_ref, pmask_ref, o_ref, state_ref,
               *, layer, rows):
    C, Dh = HGRN_CHUNK, HGRN_DIM
    nlev = len(HGRN_LEVELS)

    @pl.when(pl.program_id(1) == 0)
    def _():
        state_ref[...] = jnp.zeros(state_ref.shape, F32)

    raw = lbraw_ref[...]
    ex = jnp.exp(raw - jnp.max(raw, axis=0, keepdims=True))
    sm = ex / jnp.sum(ex, axis=0, keepdims=True)
    cs = sm[0:1]
    for k in range(1, layer + 1):
        cs = cs + sm[k:k + 1]
    lb = cs - sm[0:1]
    log_lb = jnp.log(jnp.maximum(lb, LOG_FLOOR))
    log_1m = jnp.log1p(-lb)
    W = HGRN_HEADS * Dh
    row = lax.broadcasted_iota(jnp.int32, (C, HGRN_SUB * W), 0)
    amat = amat_ref[...]
    wide = lambda t: jnp.concatenate([t] * HGRN_SUB, axis=1)
    lb_w, log_lb_w, log_1m_w = wide(lb), wide(log_lb), wide(log_1m)
    blocks = [slice(n * Dh, (n + 1) * Dh) for n in range(HGRN_SUB * HGRN_HEADS)]

    def step(ci, carry):
        rws = [pl.ds(pl.multiple_of((ci * HGRN_SUB + u) * C, C), C) for u in range(HGRN_SUB)]
        side = lambda ref: jnp.concatenate([ref[r, :] for r in rws], axis=1)
        z, q, gg = side(f_ref), side(q_ref), side(g_ref)
        v = side(i_ref).astype(BF16)
        e = jnp.exp(-jnp.abs(z))
        inv = 1.0 / (1.0 + e)
        c = log_1m_w + (jnp.minimum(z, 0.0) - jnp.log(1.0 + e))
        lf = jnp.maximum(log_lb_w, c) + jnp.log(1.0 + jnp.exp(-jnp.abs(log_lb_w - c)))
        k = (1.0 - lb_w) * jnp.where(z > 0.0, e * inv, inv)
        dsum = _dot(amat, jnp.concatenate(_split3(lf), axis=0))
        qb, kb = q.astype(BF16), k.astype(BF16)
        atts = [pmask_ref[nlev] * _dot_nt(qb[:, s], kb[:, s]) for s in blocks]
        for lev, m in enumerate(HGRN_LEVELS):
            dec = jnp.exp(dsum[lev * C:(lev + 1) * C])
            if m % 8 == 0:
                runs = [slice(r0, r0 + m) for r0 in range(0, C, m)]
                zeros = jnp.zeros((m, q.shape[1]), F32)
                qt = jnp.concatenate([q[r] * dec[r] if n % 2 else zeros for n, r in enumerate(runs)], axis=0)
                kt = jnp.concatenate([zeros if n % 2 else k[r] * dec[r] for n, r in enumerate(runs)], axis=0)
                qt, kt = qt.astype(BF16), kt.astype(BF16)
            else:
                upper = (row // m) % 2 == 1
                qt = (jnp.where(upper, q, 0.0) * dec).astype(BF16)
                kt = (jnp.where(upper, 0.0, k) * dec).astype(BF16)
            pm = pmask_ref[lev]
            atts = [att + pm * _dot_nt(qt[:, s], kt[:, s]) for att, s in zip(atts, blocks)]
        b = dsum[nlev * C:(nlev + 1) * C]
        rb = dsum[(nlev + 1) * C:(nlev + 2) * C]
        qhat = (q * jnp.exp(b)).astype(BF16)
        khat = (k * jnp.exp(rb)).astype(BF16)
        dlast = jnp.exp(b[C - 1:C, :])
        intra = [_dot(att.astype(BF16), v[:, s]) for att, s in zip(atts, blocks)]
        update = [_dot_tn(v[:, s], khat[:, s]) for s in blocks]
        states = [state_ref[h] for h in range(HGRN_HEADS)]
        outs = []
        for u in range(HGRN_SUB):
            blk = blocks[u * HGRN_HEADS:(u + 1) * HGRN_HEADS]
            outs += [intra[u * HGRN_HEADS + h] + _dot_nt(qhat[:, s], states[h].astype(BF16)) for h, s in enumerate(blk)]
            states = [st * dlast[:, s] + update[u * HGRN_HEADS + h] for h, (s, st) in enumerate(zip(blk, states))]
        for h, st in enumerate(states):
            state_ref[h] = st
        outs = [o * lax.rsqrt(jnp.mean(o * o, axis=-1, keepdims=True) + RMS_EPS) for o in outs]
        gated = jnp.concatenate(outs, axis=1) * wide(ng_ref[...]) * (gg * jax.nn.sigmoid(gg))
        for u, r in enumerate(rws):
            o_ref[r, :] = gated[:, u * W:(u + 1) * W].astype(o_ref.dtype)
        return carry

    lax.fori_loop(0, rows // (C * HGRN_SUB), step, 0)


def _hgrn(hf, lb_raw, norm_g, amat, pmask, layer, batch, seq, rows=512):
    T = hf.shape[0]
    W = HGRN_HEADS * HGRN_DIM
    ns = seq // rows
    spec = lambda cb: pl.BlockSpec((rows, W), lambda b, i: (b * ns + i, cb))
    return pl.pallas_call(
        functools.partial(_hgrn_body, layer=layer, rows=rows),
        grid=(batch, ns),
        in_specs=[spec(0), spec(1), spec(2), spec(3),
                  pl.BlockSpec(lb_raw.shape, lambda b, i: (0, 0)), pl.BlockSpec((None, 1, W), lambda b, i: (layer, 0, 0)),
                  pl.BlockSpec(amat.shape, lambda b, i: (0, 0)), pl.BlockSpec(pmask.shape, lambda b, i: (0, 0, 0))],
        out_specs=pl.BlockSpec((rows, W), lambda b, i: (b * ns + i, 0)),
        out_shape=jax.ShapeDtypeStruct((T, W), BF16),
        scratch_shapes=[pltpu.VMEM((HGRN_HEADS, HGRN_DIM, HGRN_DIM), F32)],
        compiler_params=_params(("parallel", "arbitrary")),
        name="hgrn2",
    )(hf, hf, hf, hf, lb_raw, norm_g, amat, pmask)


def _rope_tables(seq):
    inv = 1.0 / (ROPE_THETA ** (jnp.arange(0, HEAD_DIM, 2, dtype=F32) / HEAD_DIM))
    ang = jnp.arange(seq, dtype=F32)[:, None] * inv[None, :]
    cos, sin = jnp.cos(ang), jnp.sin(ang)
    return jnp.tile(cos, (1, 4)), jnp.concatenate([-sin, sin, -sin, sin], axis=1)


def kernel(x, p, ln_g, ln_b, ffn_w_gu, ffn_w_down, w_in, w_out, cmp_pos, cmp_k_w1, cmp_k_w2, cmp_v_w1, cmp_v_w2,
           swa_sinks, hgrn_lb_raw, hgrn_norm_g, ple_w, ple_gate_w, ple_gate_b):
    B, S, D = x.shape
    depth = w_in.shape[0]
    T = B * S
    alpha = (2.0 * depth) ** 0.25
    cosf, sinf = _rope_tables(S)
    ovl = _overlap_t(S)
    amat, pmask = _hgrn_tables()
    dup = lambda w: jnp.concatenate([w, w], axis=2).astype(BF16)
    bf = lambda w: w.astype(BF16)
    wgu, wd, wo, pw, gw = bf(ffn_w_gu), bf(ffn_w_down), bf(w_out), bf(ple_w), bf(ple_gate_w)
    wk1, wk2, wv1, wv2 = bf(cmp_k_w1), dup(cmp_k_w2), bf(cmp_v_w1), dup(cmp_v_w2)
    w_proj = _pack_w_in(w_in)
    lng, lnb = ln_g.reshape(depth, 3, 1, D), ln_b.reshape(depth, 3, 1, D)
    gate_b = ple_gate_b.reshape(depth, 1, D)
    norm_g = hgrn_norm_g.reshape(depth, 1, -1)
    pos2 = cmp_pos.reshape(depth, 2, CMP_STRIDE * HEAD_DIM)
    p3 = p.reshape(depth, T, -1)
    n16 = S // CMP_STRIDE
    xt = x.reshape(T, D)
    for i in range(depth):
        xt = _ffn_ln(xt, _pick(wgu, i, 0), _pick(wd, i, 0), _pick(lng, i, 0), _pick(lnb, i, 0), alpha)
        a, hf, kcm, vcm, gates = _proj_in(xt, _pick(w_proj, i), cosf, sinf, S)
        kc, vc = _compress(kcm.reshape(B, n16, CMP_STRIDE * HEAD_DIM), vcm.reshape(B, n16, CMP_STRIDE * HEAD_DIM),
                           _pick(pos2, i), _pick(wk1, i), _pick(wk2, i), _pick(wv1, i), _pick(wv2, i))
        o_nsa = _nsa_attention(a, gates, kc, vc, ovl, B, S)
        o_swa = _swa_attention(a, swa_sinks, i, B, S)
        o_hgrn = _hgrn(hf, hgrn_lb_raw, norm_g, amat, pmask, i, B, S)
        xt = _ffn_ln(xt, _pick(wgu, i, 1), _pick(wd, i, 1), _pick(lng, i, 2), _pick(lnb, i, 2), alpha,
                     mix=(o_nsa, o_swa, o_hgrn, _pick(wo, i), _pick(lng, i, 1), _pick(lnb, i, 1)),
                     ple=(_pick(p3, i), _pick(pw, i), _pick(gw, i), _pick(gate_b, i)))
    return xt.reshape(B, S, D)
```
